```python
import jax, jax.numpy as jnp
from jax import lax
import numpy as np

D_MODEL = 2048
BATCH = 4
SEQ = 8192
DEPTH = 1
DEC_BATCH = 4
DEC_SEQ = 4096
PAST_LEN = 128

HEAD_DIM = 128
N_Q_HEADS = 8
N_KV_HEADS = 2
ATTN_WIDTH = N_Q_HEADS * HEAD_DIM
KV_WIDTH = N_KV_HEADS * HEAD_DIM
CONV_WIDTH = 512
CONV_K = 3
N_MEM_HEADS = 4
MEM_WIDTH = N_MEM_HEADS * HEAD_DIM
MEM_TOKENS = 256
MIX_WIDTH = ATTN_WIDTH + CONV_WIDTH + MEM_WIDTH
IN_WIDTH = ATTN_WIDTH + 2 * KV_WIDTH + 3 * CONV_WIDTH + MEM_WIDTH
GRID_W = 64
ROPE_AXIS_DIM = HEAD_DIM // 2
ROPE_THETA = 10000.0
Q_BLOCK = 128
N_GROUPS = 8
EXPERTS_PER_GROUP = 8
N_EXPERTS = N_GROUPS * EXPERTS_PER_GROUP
TOP_K = 2
D_FF_EXPERT = 512
DISPATCH_BLOCK = 256
EPS = 1e-6

kernel_name = 'hybrid_parallel_group_encoder_hmoe'


def rms_norm(x, g):
    x32 = x.astype(jnp.float32)
    y = x32 * lax.rsqrt(jnp.mean(x32 * x32, axis=-1, keepdims=True) + EPS)
    return (y * g.astype(jnp.float32)).astype(x.dtype)


def axial_rope_tables(seq_len):
    rows = seq_len // GRID_W
    row = jnp.repeat(jnp.arange(rows, dtype=jnp.float32), GRID_W)
    col = jnp.tile(jnp.arange(GRID_W, dtype=jnp.float32), rows)
    inv_freq = ROPE_THETA ** (-jnp.arange(0, ROPE_AXIS_DIM, 2, dtype=jnp.float32) / ROPE_AXIS_DIM)
    ang_row = row[:, None] * inv_freq
    ang_col = col[:, None] * inv_freq
    return jnp.cos(ang_row), jnp.sin(ang_row), jnp.cos(ang_col), jnp.sin(ang_col)


def rotate_half_rope(x, cos, sin):
    x1, x2 = jnp.split(x, 2, axis=-1)
    c = cos[:, None, :].astype(x.dtype)
    s = sin[:, None, :].astype(x.dtype)
    return jnp.concatenate([x1 * c - x2 * s, x2 * c + x1 * s], axis=-1)


def apply_axial_rope(x, tables):
    cos_r, sin_r, cos_c, sin_c = tables
    x_row, x_col = jnp.split(x, 2, axis=-1)
    return jnp.concatenate([rotate_half_rope(x_row, cos_r, sin_r),
                            rotate_half_rope(x_col, cos_c, sin_c)], axis=-1)


def blocked_gqa_attention(q, k, v):
    b, s, h, dh = q.shape
    grp = h // N_KV_HEADS
    nb = s // Q_BLOCK
    qb = q.reshape(b, nb, Q_BLOCK, N_KV_HEADS, grp, dh).transpose(1, 0, 2, 3, 4, 5)
    scale = HEAD_DIM ** -0.5

    def one_block(qblk):
        sc = jnp.einsum('bqkgd,bskd->bkgqs', qblk, k, preferred_element_type=jnp.float32) * scale
        p = jax.nn.softmax(sc, axis=-1).astype(v.dtype)
        return jnp.einsum('bkgqs,bskd->bqkgd', p, v)

    ob = lax.map(one_block, qb)
    return ob.transpose(1, 0, 2, 3, 4, 5).reshape(b, s, h * dh)


def short_gated_conv(cx, cb, cc, conv_w, conv_b):
    u = cc * cx
    up = jnp.pad(u, ((0, 0), (1, 1), (0, 0)))
    conv = up[:, :-2] * conv_w[0] + up[:, 1:-1] * conv_w[1] + up[:, 2:] * conv_w[2] + conv_b
    return cb * conv


def memory_cross_attention(mq, mem, g_mem, w_mem_kv, g_mq, g_mk):
    b, s, _ = mq.shape
    m = rms_norm(mem, g_mem)
    mkv = m @ w_mem_kv
    mk, mv = jnp.split(mkv, 2, axis=-1)
    n_mem = mem.shape[1]
    mk = rms_norm(mk.reshape(b, n_mem, N_MEM_HEADS, HEAD_DIM), g_mk)
    mv = mv.reshape(b, n_mem, N_MEM_HEADS, HEAD_DIM)
    q = rms_norm(mq.reshape(b, s, N_MEM_HEADS, HEAD_DIM), g_mq)
    sc = jnp.einsum('bshd,bmhd->bhsm', q, mk, preferred_element_type=jnp.float32) * (HEAD_DIM ** -0.5)
    p = jax.nn.softmax(sc, axis=-1).astype(mv.dtype)
    return jnp.einsum('bhsm,bmhd->bshd', p, mv).reshape(b, s, MEM_WIDTH)


def hierarchical_route(h, w_route_group, w_route_expert):
    m = h.shape[0]
    group_probs = jax.nn.softmax(jnp.matmul(h, w_route_group, preferred_element_type=jnp.float32), axis=-1)
    g_idx = jnp.argmax(group_probs, axis=-1)
    p_group = jnp.take_along_axis(group_probs, g_idx[:, None], axis=-1)
    logits = jnp.matmul(h, w_route_expert, preferred_element_type=jnp.float32).reshape(m, N_GROUPS, EXPERTS_PER_GROUP)
    in_group = jnp.take_along_axis(logits, g_idx[:, None, None], axis=1)[:, 0]
    top_p, top_j = lax.top_k(jax.nn.softmax(in_group, axis=-1), TOP_K)
    gates = p_group * top_p / jnp.sum(top_p, axis=-1, keepdims=True)
    experts = (g_idx[:, None] * EXPERTS_PER_GROUP + top_j).astype(jnp.int32)
    return experts, gates


def routed_experts(h, experts, gates, w_gate, w_up, w_down):
    m, d = h.shape
    n_slots = m * TOP_K
    flat_e = experts.reshape(-1)
    order = jnp.argsort(flat_e)
    sorted_e = flat_e[order]
    counts = jnp.bincount(flat_e, length=N_EXPERTS)
    padded_counts = ((counts + DISPATCH_BLOCK - 1) // DISPATCH_BLOCK) * DISPATCH_BLOCK
    padded_end = jnp.cumsum(padded_counts)
    padded_start = padded_end - padded_counts
    start = jnp.cumsum(counts) - counts
    dest = padded_start[sorted_e] + (jnp.arange(n_slots) - start[sorted_e])
    n_blocks = -(-n_slots // DISPATCH_BLOCK) + N_EXPERTS
    token_of_slot = order // TOP_K
    xs = jnp.zeros((n_blocks * DISPATCH_BLOCK, d), h.dtype).at[dest].set(h[token_of_slot])
    block_start = jnp.arange(n_blocks) * DISPATCH_BLOCK
    block_expert = jnp.minimum(jnp.searchsorted(padded_end, block_start, side='right'), N_EXPERTS - 1)

    def block_fn(args):
        xb, e = args
        hid = jax.nn.silu(xb @ w_gate[e]) * (xb @ w_up[e])
        return hid @ w_down[e]

    ys = lax.map(block_fn, (xs.reshape(n_blocks, DISPATCH_BLOCK, d), block_expert)).reshape(-1, d)
    y_slots = ys[dest].astype(jnp.float32) * gates.reshape(-1)[order][:, None]
    out = jnp.zeros((m, d), jnp.float32).at[token_of_slot].add(y_slots)
    return out.astype(h.dtype)


def hybrid_layer(x, mem, g_mix, w_in, g_q, g_k, conv_w, conv_b, g_mem, w_mem_kv, g_mq, g_mk,
                 g_out, w_o, g_ffn, w_route_group, w_route_expert, w_gate, w_up, w_down):
    b, s, d = x.shape
    h = rms_norm(x, g_mix)
    proj = h @ w_in
    cuts = [ATTN_WIDTH, ATTN_WIDTH + KV_WIDTH, ATTN_WIDTH + 2 * KV_WIDTH,
            ATTN_WIDTH + 2 * KV_WIDTH + CONV_WIDTH, ATTN_WIDTH + 2 * KV_WIDTH + 2 * CONV_WIDTH,
            ATTN_WIDTH + 2 * KV_WIDTH + 3 * CONV_WIDTH]
    q, k, v, cx, cb, cc, mq = jnp.split(proj, cuts, axis=-1)

    tables = axial_rope_tables(s)
    q = apply_axial_rope(rms_norm(q.reshape(b, s, N_Q_HEADS, HEAD_DIM), g_q), tables)
    k = apply_axial_rope(rms_norm(k.reshape(b, s, N_KV_HEADS, HEAD_DIM), g_k), tables)
    v = v.reshape(b, s, N_KV_HEADS, HEAD_DIM)
    y_attn = blocked_gqa_attention(q, k, v)

    y_conv = short_gated_conv(cx, cb, cc, conv_w, conv_b)

    y_mem = memory_cross_attention(mq, mem, g_mem, w_mem_kv, g_mq, g_mk)

    ga, gc, gm = jnp.split(g_out, [ATTN_WIDTH, ATTN_WIDTH + CONV_WIDTH])
    mixed = jnp.concatenate([rms_norm(y_attn, ga), rms_norm(y_conv, gc), rms_norm(y_mem, gm)], axis=-1)
    x = x + mixed @ w_o

    h2 = rms_norm(x, g_ffn).reshape(b * s, d)
    experts, gates = hierarchical_route(h2, w_route_group, w_route_expert)
    x = x + routed_experts(h2, experts, gates, w_gate, w_up, w_down).reshape(b, s, d)
    return x


def run_trunk(x, mem, g_mix, w_in, g_q, g_k, conv_w, conv_b, g_mem, w_mem_kv, g_mq, g_mk,
              g_out, w_o, g_ffn, w_route_group, w_route_expert, w_gate, w_up, w_down):
    for l in range(DEPTH):
        x = hybrid_layer(x, mem, g_mix[l], w_in[l], g_q[l], g_k[l], conv_w[l], conv_b[l], g_mem[l],
                         w_mem_kv[l], g_mq[l], g_mk[l], g_out[l], w_o[l], g_ffn[l],
                         w_route_group[l], w_route_expert[l], w_gate[l], w_up[l], w_down[l])
    return x


def setup_inputs(seed: int = 0) -> dict:
    key = jax.random.key(seed)
    ks = jax.random.split(key, 24)
    f32 = jnp.float32
    L = DEPTH

    def nrm(k, shape, scale):
        return jax.random.normal(k, shape, f32) * scale

    def gain(k, shape):
        return 1.0 + 0.02 * jax.random.normal(k, shape, f32)

    return {
        'x_prompt': nrm(ks[0], (BATCH, SEQ, D_MODEL), 1.0),
        'x_sample': nrm(ks[1], (DEC_BATCH, DEC_SEQ, D_MODEL), 1.0),
        'mem_prompt': nrm(ks[2], (BATCH, MEM_TOKENS, D_MODEL), 1.0),
        'mem_sample': nrm(ks[3], (DEC_BATCH, MEM_TOKENS, D_MODEL), 1.0),
        'g_mix': gain(ks[4], (L, D_MODEL)),
        'w_in': nrm(ks[5], (L, D_MODEL, IN_WIDTH), D_MODEL ** -0.5),
        'g_q': gain(ks[6], (L, HEAD_DIM)),
        'g_k': gain(ks[7], (L, HEAD_DIM)),
        'conv_w': nrm(ks[8], (L, CONV_K, CONV_WIDTH), CONV_K ** -0.5),
        'conv_b': nrm(ks[9], (L, CONV_WIDTH), 0.01),
        'g_mem': gain(ks[10], (L, D_MODEL)),
        'w_mem_kv': nrm(ks[11], (L, D_MODEL, 2 * MEM_WIDTH), D_MODEL ** -0.5),
        'g_mq': gain(ks[12], (L, HEAD_DIM)),
        'g_mk': gain(ks[13], (L, HEAD_DIM)),
        'g_out': gain(ks[14], (L, MIX_WIDTH)),
        'w_o': nrm(ks[15], (L, MIX_WIDTH, D_MODEL), MIX_WIDTH ** -0.5),
        'g_ffn': gain(ks[16], (L, D_MODEL)),
        'w_route_group': nrm(ks[17], (L, D_MODEL, N_GROUPS), D_MODEL ** -0.5),
        'w_route_expert': nrm(ks[18], (L, D_MODEL, N_EXPERTS), D_MODEL ** -0.5),
        'w_gate': nrm(ks[19], (L, N_EXPERTS, D_MODEL, D_FF_EXPERT), D_MODEL ** -0.5),
        'w_up': nrm(ks[20], (L, N_EXPERTS, D_MODEL, D_FF_EXPERT), D_MODEL ** -0.5),
        'w_down': nrm(ks[21], (L, N_EXPERTS, D_FF_EXPERT, D_MODEL), D_FF_EXPERT ** -0.5),
    }


def reference(x_prompt, x_sample, mem_prompt, mem_sample, g_mix, w_in, g_q, g_k, conv_w, conv_b,
              g_mem, w_mem_kv, g_mq, g_mk, g_out, w_o, g_ffn, w_route_group, w_route_expert,
              w_gate, w_up, w_down):
    y_prompt = run_trunk(x_prompt, mem_prompt, g_mix, w_in, g_q, g_k, conv_w, conv_b, g_mem, w_mem_kv,
                         g_mq, g_mk, g_out, w_o, g_ffn, w_route_group, w_route_expert, w_gate, w_up, w_down)
    y_sample = run_trunk(x_sample, mem_sample, g_mix, w_in, g_q, g_k, conv_w, conv_b, g_mem, w_mem_kv,
                         g_mq, g_mk, g_out, w_o, g_ffn, w_route_group, w_route_expert, w_gate, w_up, w_down)
    return (y_prompt, y_sample)
```

```python
import functools

import jax
import jax.numpy as jnp
from jax import lax
from jax.experimental import pallas as pl
from jax.experimental.pallas import tpu as pltpu

D_MODEL = 2048
HEAD_DIM = 128
N_Q_HEADS = 8
N_KV_HEADS = 2
Q_PER_KV = N_Q_HEADS // N_KV_HEADS
ATTN_WIDTH = N_Q_HEADS * HEAD_DIM
KV_WIDTH = N_KV_HEADS * HEAD_DIM
CONV_WIDTH = 512
N_MEM_HEADS = 4
MEM_WIDTH = N_MEM_HEADS * HEAD_DIM
MIX_WIDTH = ATTN_WIDTH + CONV_WIDTH + MEM_WIDTH
GRID_W = 64
ROPE_AXIS_DIM = HEAD_DIM // 2
ROPE_THETA = 10000.0
N_GROUPS = 8
EXPERTS_PER_GROUP = 8
N_EXPERTS = N_GROUPS * EXPERTS_PER_GROUP
TOP_K = 2
D_FF_EXPERT = 512
DISPATCH_BLOCK = 256
EPS = 1e-6

ROUTE_LANES = 128
V7X_VMEM_LIMIT = 56 * 1024 * 1024

F32 = jnp.float32
BF16 = jnp.bfloat16
NEG_BIG = -1e30


def _rms(x, g):
    return x * lax.rsqrt(jnp.mean(x * x, axis=-1, keepdims=True) + EPS) * g


def _cparams(sem, **kw):
    return pltpu.CompilerParams(dimension_semantics=sem, vmem_limit_bytes=V7X_VMEM_LIMIT, **kw)


def _mem_kv_kernel(mem_ref, g_ref, w_ref, gk_ref, mk_ref, mv_ref):
    m = _rms(mem_ref[0], g_ref[...]).astype(BF16)
    kv = jnp.dot(m, w_ref[...], preferred_element_type=F32)
    for h in range(N_MEM_HEADS):
        sl = slice(h * HEAD_DIM, (h + 1) * HEAD_DIM)
        mk_ref[0, :, sl] = _rms(kv[:, sl], gk_ref[...]).astype(BF16)
    mv_ref[0] = kv[:, MEM_WIDTH:].astype(BF16)


def _mem_kv(mem, g_mem, w_mem_kv, g_mk):
    b, n_mem, d = mem.shape
    out = jax.ShapeDtypeStruct((b, n_mem, MEM_WIDTH), BF16)
    return pl.pallas_call(
        _mem_kv_kernel,
        out_shape=(out, out),
        grid=(b,),
        in_specs=[
            pl.BlockSpec((1, n_mem, d), lambda i: (i, 0, 0)),
            pl.BlockSpec((1, d), lambda i: (0, 0)),
            pl.BlockSpec((d, 2 * MEM_WIDTH), lambda i: (0, 0)),
            pl.BlockSpec((1, HEAD_DIM), lambda i: (0, 0)),
        ],
        out_specs=(pl.BlockSpec((1, n_mem, MEM_WIDTH), lambda i: (i, 0, 0)),
                   pl.BlockSpec((1, n_mem, MEM_WIDTH), lambda i: (i, 0, 0))),
        compiler_params=_cparams(("arbitrary",)),
        name="mem_kv",
    )(mem, g_mem.reshape(1, d), w_mem_kv, g_mk.reshape(1, HEAD_DIM))


def _rope_tables(seq_len):
    t = jnp.arange(seq_len, dtype=jnp.int32)
    row = (t // GRID_W).astype(F32)
    col = (t % GRID_W).astype(F32)
    inv_freq = ROPE_THETA ** (-jnp.arange(0, ROPE_AXIS_DIM, 2, dtype=F32) / ROPE_AXIS_DIM)
    ang_r = row[:, None] * inv_freq
    ang_c = col[:, None] * inv_freq
    cos = jnp.concatenate([jnp.cos(ang_r), jnp.cos(ang_r), jnp.cos(ang_c), jnp.cos(ang_c)], axis=-1)
    sin = jnp.concatenate([-jnp.sin(ang_r), jnp.sin(ang_r), -jnp.sin(ang_c), jnp.sin(ang_c)], axis=-1)
    return cos, sin


def _in_proj_kernel(x_ref, gmix_ref, w_ref, gq_ref, gk_ref, gmq_ref, cos_ref, sin_ref,
                    q_ref, k_ref, v_ref, u_ref, cb_ref, mq_ref):
    tm = x_ref.shape[0]
    hb = _rms(x_ref[...], gmix_ref[...]).astype(BF16)
    cos = cos_ref[...]
    sin = sin_ref[...]
    lane = lax.broadcasted_iota(jnp.int32, (tm, HEAD_DIM), 1)
    first_half = (lane % (ROPE_AXIS_DIM)) < (ROPE_AXIS_DIM // 2)
    scale = HEAD_DIM ** -0.5

    def rope(p):
        swapped = jnp.where(first_half, pltpu.roll(p, HEAD_DIM - ROPE_AXIS_DIM // 2, 1),
                            pltpu.roll(p, ROPE_AXIS_DIM // 2, 1))
        return p * cos + swapped * sin

    def proj(lo, width):
        return jnp.dot(hb, w_ref[:, lo:lo + width], preferred_element_type=F32)

    pq = proj(0, ATTN_WIDTH)
    for h in range(N_Q_HEADS):
        sl = slice(h * HEAD_DIM, (h + 1) * HEAD_DIM)
        q_ref[:, sl] = (rope(_rms(pq[:, sl], gq_ref[...])) * scale).astype(BF16)
    pkv = proj(ATTN_WIDTH, 2 * KV_WIDTH)
    for h in range(N_KV_HEADS):
        sl = slice(h * HEAD_DIM, (h + 1) * HEAD_DIM)
        k_ref[:, sl] = rope(_rms(pkv[:, sl], gk_ref[...])).astype(BF16)
    v_ref[...] = pkv[:, KV_WIDTH:].astype(BF16)
    pc = proj(ATTN_WIDTH + 2 * KV_WIDTH, 3 * CONV_WIDTH)
    u_ref[...] = (pc[:, 2 * CONV_WIDTH:] * pc[:, :CONV_WIDTH]).astype(BF16)
    cb_ref[...] = pc[:, CONV_WIDTH:2 * CONV_WIDTH].astype(BF16)
    pm = proj(ATTN_WIDTH + 2 * KV_WIDTH + 3 * CONV_WIDTH, MEM_WIDTH)
    for h in range(N_MEM_HEADS):
        sl = slice(h * HEAD_DIM, (h + 1) * HEAD_DIM)
        mq_ref[:, sl] = (_rms(pm[:, sl], gmq_ref[...]) * scale).astype(BF16)


def _in_proj(x2d, seq_len, g_mix, w_in_b, g_q, g_k, g_mq, cos, sin, tm):
    m, d = x2d.shape
    in_width = w_in_b.shape[1]
    tiles_per_seq = seq_len // tm
    row = lambda i: (i, 0)
    const = lambda i: (0, 0)
    pos = lambda i: (i % tiles_per_seq, 0)
    outs = [(ATTN_WIDTH,), (KV_WIDTH,), (KV_WIDTH,), (CONV_WIDTH,), (CONV_WIDTH,), (MEM_WIDTH,)]
    return pl.pallas_call(
        _in_proj_kernel,
        out_shape=tuple(jax.ShapeDtypeStruct((m, w[0]), BF16) for w in outs),
        grid=(m // tm,),
        in_specs=[
            pl.BlockSpec((tm, d), row),
            pl.BlockSpec((1, d), const),
            pl.BlockSpec((d, in_width), const, pipeline_mode=pl.Buffered(1)),
            pl.BlockSpec((1, HEAD_DIM), const),
            pl.BlockSpec((1, HEAD_DIM), const),
            pl.BlockSpec((1, HEAD_DIM), const),
            pl.BlockSpec((tm, HEAD_DIM), pos),
            pl.BlockSpec((tm, HEAD_DIM), pos),
        ],
        out_specs=tuple(pl.BlockSpec((tm, w[0]), row) for w in outs),
        compiler_params=_cparams(("parallel",)),
        name="in_proj",
    )(x2d, g_mix.reshape(1, d), w_in_b, g_q.reshape(1, HEAD_DIM), g_k.reshape(1, HEAD_DIM),
      g_mq.reshape(1, HEAD_DIM), cos, sin)


def _attn_kernel(q_ref, k_ref, v_ref, o_ref, *, tk):
    tq = q_ref.shape[1]
    seq = k_ref.shape[1]
    rows = Q_PER_KV * tq
    q = jnp.concatenate([q_ref[0, :, g * HEAD_DIM:(g + 1) * HEAD_DIM] for g in range(Q_PER_KV)], axis=0)

    def body(j, carry):
        m, l, acc = carry
        start = pl.multiple_of(j * tk, tk)
        kc = k_ref[0, pl.ds(start, tk), :]
        vc = v_ref[0, pl.ds(start, tk), :]
        s = lax.dot_general(q, kc, (((1,), (1,)), ((), ())), preferred_element_type=F32)
        m_new = jnp.maximum(m, jnp.max(s, axis=1, keepdims=True))
        alpha = jnp.exp(m - m_new)
        p = jnp.exp(s - m_new)
        l = alpha * l + jnp.sum(p, axis=1, keepdims=True)
        acc = alpha * acc + jnp.dot(p.astype(BF16), vc, preferred_element_type=F32)
        return m_new, l, acc

    init = (jnp.full((rows, 1), NEG_BIG, F32), jnp.zeros((rows, 1), F32), jnp.zeros((rows, HEAD_DIM), F32))
    _, l, acc = lax.fori_loop(0, seq // tk, body, init)
    o = acc / l
    for g in range(Q_PER_KV):
        o_ref[0, :, g * HEAD_DIM:(g + 1) * HEAD_DIM] = o[g * tq:(g + 1) * tq].astype(BF16)


def _attention(q, k, v, tq, tk):
    b, s, _ = q.shape
    grp_w = Q_PER_KV * HEAD_DIM
    return pl.pallas_call(
        functools.partial(_attn_kernel, tk=tk),
        out_shape=jax.ShapeDtypeStruct((b, s, ATTN_WIDTH), BF16),
        grid=(b, N_KV_HEADS, s // tq),
        in_specs=[
            pl.BlockSpec((1, tq, grp_w), lambda bi, h, qi: (bi, qi, h)),
            pl.BlockSpec((1, s, HEAD_DIM), lambda bi, h, qi: (bi, 0, h)),
            pl.BlockSpec((1, s, HEAD_DIM), lambda bi, h, qi: (bi, 0, h)),
        ],
        out_specs=pl.BlockSpec((1, tq, grp_w), lambda bi, h, qi: (bi, qi, h)),
        compiler_params=_cparams(("parallel", "parallel", "parallel")),
        name="attention",
    )(q, k, v)


def _route(logits):
    tm = logits.shape[0]
    lane = lax.broadcasted_iota(jnp.int32, (tm, ROUTE_LANES), 1)
    lane_f = lane.astype(F32)
    no_lane = float(ROUTE_LANES)
    is_g = lane < N_GROUPS
    gl = jnp.where(is_g, logits, NEG_BIG)
    gmax = jnp.max(gl, axis=1, keepdims=True)
    gsum = jnp.sum(jnp.where(is_g, jnp.exp(gl - gmax), 0.0), axis=1, keepdims=True)
    p_group = 1.0 / gsum
    gidx = jnp.min(jnp.where(is_g & (gl == gmax), lane_f, no_lane), axis=1, keepdims=True)
    lo = N_GROUPS + EXPERTS_PER_GROUP * gidx
    in_grp = (lane_f >= lo) & (lane_f < lo + EXPERTS_PER_GROUP)
    el = jnp.where(in_grp, logits, NEG_BIG)
    emax = jnp.max(el, axis=1, keepdims=True)
    j1 = jnp.min(jnp.where(in_grp & (el == emax), lane_f, no_lane), axis=1, keepdims=True)
    rest = in_grp & (lane_f != j1)
    el2 = jnp.where(rest, logits, NEG_BIG)
    e2max = jnp.max(el2, axis=1, keepdims=True)
    j2 = jnp.min(jnp.where(rest & (el2 == e2max), lane_f, no_lane), axis=1, keepdims=True)
    r = jnp.exp(e2max - emax)
    g1 = p_group / (1.0 + r)
    g2 = g1 * r
    info = jnp.where(lane == 0, j1 - N_GROUPS, 0.0)
    info = jnp.where(lane == 1, j2 - N_GROUPS, info)
    info = jnp.where(lane == 2, g1, info)
    info = jnp.where(lane == 3, g2, info)
    return info


def _split_bf16(a):
    hi = a.astype(BF16)
    lo = (a - hi.astype(F32)).astype(BF16)
    return hi, lo


def _mix_out_kernel(ya_ref, u_ref, up_ref, un_ref, cb_ref, mq_ref, mk_ref, mv_ref, x_ref,
                    wo_ref, gout_ref, convw_ref, convb_ref, gffn_ref, wr_hi_ref, wr_lo_ref,
                    x1_ref, h2_ref, info_ref, *, tiles_per_seq):
    tm = x_ref.shape[0]
    i = pl.program_id(0)
    t_in_seq = i % tiles_per_seq

    a = _rms(ya_ref[...].astype(F32), gout_ref[:, :ATTN_WIDTH])

    u = u_ref[...].astype(F32)
    rowi = lax.broadcasted_iota(jnp.int32, (tm, CONV_WIDTH), 0)
    prev_row = jnp.where(t_in_seq == 0, 0.0, up_ref[7:8, :].astype(F32))
    next_row = jnp.where(t_in_seq == tiles_per_seq - 1, 0.0, un_ref[0:1, :].astype(F32))
    u_prev = jnp.where(rowi == 0, prev_row, pltpu.roll(u, 1, 0))
    u_next = jnp.where(rowi == tm - 1, next_row, pltpu.roll(u, tm - 1, 0))
    conv = u_prev * convw_ref[0:1, :] + u * convw_ref[1:2, :] + u_next * convw_ref[2:3, :] + convb_ref[...]
    c = _rms(cb_ref[...].astype(F32) * conv, gout_ref[:, ATTN_WIDTH:ATTN_WIDTH + CONV_WIDTH])

    heads = []
    for h in range(N_MEM_HEADS):
        sl = slice(h * HEAD_DIM, (h + 1) * HEAD_DIM)
        s = lax.dot_general(mq_ref[:, sl], mk_ref[0, :, sl], (((1,), (1,)), ((), ())),
                            preferred_element_type=F32)
        p = jnp.exp(s - jnp.max(s, axis=1, keepdims=True))
        o = jnp.dot(p.astype(BF16), mv_ref[0, :, sl], preferred_element_type=F32)
        heads.append(o / jnp.sum(p, axis=1, keepdims=True))
    ym = _rms(jnp.concatenate(heads, axis=1), gout_ref[:, ATTN_WIDTH + CONV_WIDTH:])

    mixed = jnp.concatenate([a, c, ym], axis=1).astype(BF16)
    x1 = x_ref[...] + jnp.dot(mixed, wo_ref[...], preferred_element_type=F32)
    x1_ref[...] = x1
    h2 = _rms(x1, gffn_ref[...])
    h2_ref[...] = h2
    h_hi, h_lo = _split_bf16(h2)
    logits = (jnp.dot(h_hi, wr_hi_ref[...], preferred_element_type=F32)
              + jnp.dot(h_hi, wr_lo_ref[...], preferred_element_type=F32)
              + jnp.dot(h_lo, wr_hi_ref[...], preferred_element_type=F32))
    info_ref[...] = _route(logits)


def _mix_out(y_attn, u, cb, mq, mk, mv, x2d, seq_len, w_o_b, g_out, conv_w, conv_b, g_ffn,
             wr_hi, wr_lo, tm):
    m, d = x2d.shape
    n_mem = mk.shape[1]
    tiles_per_seq = seq_len // tm
    sub = tm // 8
    n_sub = m // 8
    row = lambda i: (i, 0)
    const = lambda i: (0, 0)
    batch = lambda i: (i // tiles_per_seq, 0, 0)
    return pl.pallas_call(
        functools.partial(_mix_out_kernel, tiles_per_seq=tiles_per_seq),
        out_shape=(jax.ShapeDtypeStruct((m, d), F32), jax.ShapeDtypeStruct((m, d), F32),
                   jax.ShapeDtypeStruct((m, ROUTE_LANES), F32)),
        grid=(m // tm,),
        in_specs=[
            pl.BlockSpec((tm, ATTN_WIDTH), row),
            pl.BlockSpec((tm, CONV_WIDTH), row),
            pl.BlockSpec((8, CONV_WIDTH), lambda i: (jnp.maximum(i * sub - 1, 0), 0)),
            pl.BlockSpec((8, CONV_WIDTH), lambda i: (jnp.minimum((i + 1) * sub, n_sub - 1), 0)),
            pl.BlockSpec((tm, CONV_WIDTH), row),
            pl.BlockSpec((tm, MEM_WIDTH), row),
            pl.BlockSpec((1, n_mem, MEM_WIDTH), batch),
            pl.BlockSpec((1, n_mem, MEM_WIDTH), batch),
            pl.BlockSpec((tm, d), row),
            pl.BlockSpec((MIX_WIDTH, d), const, pipeline_mode=pl.Buffered(1)),
            pl.BlockSpec((1, MIX_WIDTH), const),
            pl.BlockSpec((3, CONV_WIDTH), const),
            pl.BlockSpec((1, CONV_WIDTH), const),
            pl.BlockSpec((1, d), const),
            pl.BlockSpec((d, ROUTE_LANES), const),
            pl.BlockSpec((d, ROUTE_LANES), const),
        ],
        out_specs=(pl.BlockSpec((tm, d), row), pl.BlockSpec((tm, d), row),
                   pl.BlockSpec((tm, ROUTE_LANES), row)),
        compiler_params=_cparams(("parallel",)),
        name="mix_out",
    )(y_attn, u, u, u, cb, mq, mk, mv, x2d, w_o_b, g_out.reshape(1, MIX_WIDTH), conv_w,
      conv_b.reshape(1, CONV_WIDTH), g_ffn.reshape(1, d), wr_hi, wr_lo)


def _row_copy(src_hbm, idx, dst, row, sem):
    return pltpu.make_async_copy(src_hbm.at[pl.ds(idx, 1), :], dst.at[pl.ds(row, 1), :], sem)


def _experts_kernel(bexp_ref, nused_ref, tok_ref, h2_hbm, wg_ref, wu_ref, wd_ref, ys_ref,
                    xbuf, sem, wg_b, wu_b, wd_b):
    i = pl.program_id(0)
    n_used = nused_ref[0]
    blk = DISPATCH_BLOCK

    def start_gather(b, slot):
        def body(r, carry):
            _row_copy(h2_hbm, tok_ref[b * blk + r], xbuf.at[slot], r, sem.at[slot]).start()
            return carry
        lax.fori_loop(0, blk, body, 0)

    @pl.when(i == 0)
    def _():
        start_gather(0, 0)

    @pl.when(i + 1 < n_used)
    def _():
        start_gather(i + 1, (i + 1) % 2)

    @pl.when(i < n_used)
    def _():
        slot = i % 2
        pltpu.make_async_copy(h2_hbm.at[pl.ds(0, blk), :], xbuf.at[slot], sem.at[slot]).wait()

        @pl.when((i == 0) | (bexp_ref[i] != bexp_ref[jnp.maximum(i - 1, 0)]))
        def _():
            wg_b[...] = wg_ref[...].astype(BF16)
            wu_b[...] = wu_ref[...].astype(BF16)
            wd_b[...] = wd_ref[...].astype(BF16)

        x = xbuf[slot].astype(BF16)
        g = jnp.dot(x, wg_b[...], preferred_element_type=F32)
        up = jnp.dot(x, wu_b[...], preferred_element_type=F32)
        hid = (g / (1.0 + jnp.exp(-g))) * up
        ys_ref[...] = jnp.dot(hid.astype(BF16), wd_b[...], preferred_element_type=F32)

    @pl.when(i >= n_used)
    def _():
        ys_ref[...] = jnp.zeros_like(ys_ref)


def _experts(block_expert, n_used, token_of_pos, h2, w_gate, w_up, w_down):
    n_blocks = block_expert.shape[0]
    d = h2.shape[1]
    blk = DISPATCH_BLOCK
    wmap = lambda i, bexp, nused, tok: (bexp[i], 0, 0)
    grid_spec = pltpu.PrefetchScalarGridSpec(
        num_scalar_prefetch=3,
        grid=(n_blocks,),
        in_specs=[
            pl.BlockSpec(memory_space=pl.ANY),
            pl.BlockSpec((None, d, D_FF_EXPERT), wmap),
            pl.BlockSpec((None, d, D_FF_EXPERT), wmap),
            pl.BlockSpec((None, D_FF_EXPERT, d), wmap),
        ],
        out_specs=pl.BlockSpec((blk, d), lambda i, bexp, nused, tok: (i, 0)),
        scratch_shapes=[
            pltpu.VMEM((2, blk, d), F32),
            pltpu.SemaphoreType.DMA((2,)),
            pltpu.VMEM((d, D_FF_EXPERT), BF16),
            pltpu.VMEM((d, D_FF_EXPERT), BF16),
            pltpu.VMEM((D_FF_EXPERT, d), BF16),
        ],
    )
    return pl.pallas_call(
        _experts_kernel,
        out_shape=jax.ShapeDtypeStruct((n_blocks * blk, d), F32),
        grid_spec=grid_spec,
        compiler_params=_cparams(("arbitrary",)),
        name="experts",
    )(block_expert, n_used, token_of_pos, h2, w_gate, w_up, w_down)


def _combine_kernel(pos_ref, x1_ref, info_ref, ys_hbm, out_ref, ybuf, sem):
    i = pl.program_id(0)
    n = pl.num_programs(0)
    tm = x1_ref.shape[0]

    def start_gather(t, slot):
        def body(r, carry):
            base = (t * tm + r) * TOP_K
            for k in range(TOP_K):
                _row_copy(ys_hbm, pos_ref[base + k], ybuf.at[slot, k], r, sem.at[slot]).start()
            return carry
        lax.fori_loop(0, tm, body, 0)

    @pl.when(i == 0)
    def _():
        start_gather(0, 0)

    @pl.when(i + 1 < n)
    def _():
        start_gather(i + 1, (i + 1) % 2)

    slot = i % 2
    for k in range(TOP_K):
        pltpu.make_async_copy(ys_hbm.at[pl.ds(0, tm), :], ybuf.at[slot, k], sem.at[slot]).wait()
    info = info_ref[...]
    out_ref[...] = (x1_ref[...] + ybuf[slot, 0] * info[:, 2:3] + ybuf[slot, 1] * info[:, 3:4])


def _combine(pos, x1, info, ys, tm):
    m, d = x1.shape
    grid_spec = pltpu.PrefetchScalarGridSpec(
        num_scalar_prefetch=1,
        grid=(m // tm,),
        in_specs=[
            pl.BlockSpec((tm, d), lambda i, pos: (i, 0)),
            pl.BlockSpec((tm, ROUTE_LANES), lambda i, pos: (i, 0)),
            pl.BlockSpec(memory_space=pl.ANY),
        ],
        out_specs=pl.BlockSpec((tm, d), lambda i, pos: (i, 0)),
        scratch_shapes=[pltpu.VMEM((2, TOP_K, tm, d), F32), pltpu.SemaphoreType.DMA((2,))],
    )
    return pl.pallas_call(
        _combine_kernel,
        out_shape=jax.ShapeDtypeStruct((m, d), F32),
        grid_spec=grid_spec,
        compiler_params=_cparams(("arbitrary",)),
        name="combine",
    )(pos, x1, info, ys)


def _dispatch_plan(experts_flat):
    n_slots = experts_flat.shape[0]
    blk = DISPATCH_BLOCK
    n_blocks = -(-n_slots // blk) + N_EXPERTS
    onehot = (experts_flat[:, None] == jnp.arange(N_EXPERTS, dtype=jnp.int32)[None, :]).astype(jnp.int32)
    csum = jnp.cumsum(onehot, axis=0)
    counts = csum[-1]
    rank = jnp.sum((csum - onehot) * onehot, axis=1)
    padded = ((counts + blk - 1) // blk) * blk
    padded_end = jnp.cumsum(padded)
    padded_start = padded_end - padded
    pos = (padded_start[experts_flat] + rank).astype(jnp.int32)
    token_of_pos = jnp.zeros((n_blocks * blk,), jnp.int32).at[pos].set(
        jnp.arange(n_slots, dtype=jnp.int32) // TOP_K)
    n_used = (padded_end[-1] // blk).astype(jnp.int32)
    block_start = jnp.arange(n_blocks, dtype=jnp.int32) * blk
    last_start = jnp.maximum(padded_end[-1] - blk, 0)
    block_expert = jnp.searchsorted(padded_end, jnp.minimum(block_start, last_start), side='right')
    block_expert = jnp.minimum(block_expert, N_EXPERTS - 1).astype(jnp.int32)
    return pos, token_of_pos, block_expert, n_used.reshape(1)


def _tile(n, pref):
    return pref if n % pref == 0 else n


def kernel(x_prompt, x_sample, mem_prompt, mem_sample, g_mix, w_in, g_q, g_k, conv_w, conv_b, g_mem, w_mem_kv, g_mq, g_mk, g_out, w_o, g_ffn, w_route_group, w_route_expert, w_gate, w_up, w_down):
    assert g_mix.shape[0] == 1, "single layer"
    d = D_MODEL
    w_in_b = w_in[0].astype(BF16)
    w_o_b = w_o[0].astype(BF16)
    w_mem_b = w_mem_kv[0].astype(BF16)
    w_route = jnp.concatenate(
        [w_route_group[0], w_route_expert[0],
         jnp.zeros((d, ROUTE_LANES - N_GROUPS - N_EXPERTS), F32)], axis=1)
    wr_hi, wr_lo = _split_bf16(w_route)

    groups = []
    for x, mem in ((x_prompt, mem_prompt), (x_sample, mem_sample)):
        b, s, _ = x.shape
        tm = _tile(s, 256)
        x2d = x.reshape(b * s, d)
        cos, sin = _rope_tables(s)
        mk, mv = _mem_kv(mem, g_mem[0], w_mem_b, g_mk[0])
        q, k, v, u, cb, mq = _in_proj(x2d, s, g_mix[0], w_in_b, g_q[0], g_k[0], g_mq[0], cos, sin, tm)
        y_attn = _attention(q.reshape(b, s, ATTN_WIDTH), k.reshape(b, s, KV_WIDTH),
                            v.reshape(b, s, KV_WIDTH), _tile(s, 128), _tile(s, 512))
        x1, h2, info = _mix_out(y_attn.reshape(b * s, ATTN_WIDTH), u, cb, mq, mk, mv, x2d, s, w_o_b,
                                g_out[0], conv_w[0], conv_b[0], g_ffn[0], wr_hi, wr_lo, tm)
        groups.append((x.shape, x1, h2, info, tm))

    h2_all = jnp.concatenate([g[2] for g in groups], axis=0)
    experts_flat = jnp.concatenate([g[3][:, :TOP_K] for g in groups], axis=0).astype(jnp.int32).reshape(-1)
    pos, token_of_pos, block_expert, n_used = _dispatch_plan(experts_flat)
    ys = _experts(block_expert, n_used, token_of_pos, h2_all, w_gate[0], w_up[0], w_down[0])

    outs = []
    slot0 = 0
    for shape, x1, _, info, tm in groups:
        m = x1.shape[0]
        out = _combine(pos[slot0:slot0 + m * TOP_K], x1, info, ys, tm)
        outs.append(out.reshape(shape))
        slot0 += m * TOP_K
    return tuple(outs)
```

```python
import functools

import jax
import jax.numpy as jnp
from jax import lax
from jax.experimental import pallas as pl
from jax.experimental.pallas import tpu as pltpu

D_MODEL = 2048
HEAD_DIM = 128
N_Q_HEADS = 8
N_KV_HEADS = 2
Q_PER_KV = N_Q_HEADS // N_KV_HEADS
ATTN_WIDTH = N_Q_HEADS * HEAD_DIM
KV_WIDTH = N_KV_HEADS * HEAD_DIM
CONV_WIDTH = 512
N_MEM_HEADS = 4
MEM_WIDTH = N_MEM_HEADS * HEAD_DIM
MIX_WIDTH = ATTN_WIDTH + CONV_WIDTH + MEM_WIDTH
GRID_W = 64
ROPE_AXIS_DIM = HEAD_DIM // 2
ROPE_THETA = 10000.0
N_GROUPS = 8
EXPERTS_PER_GROUP = 8
N_EXPERTS = N_GROUPS * EXPERTS_PER_GROUP
TOP_K = 2
D_FF_EXPERT = 512
DISPATCH_BLOCK = 256
EPS = 1e-6

ROUTE_LANES = 128
V7X_VMEM_LIMIT = 56 * 1024 * 1024

F32 = jnp.float32
BF16 = jnp.bfloat16
NEG_BIG = -1e30
LOG2_E = 1.4426950408889634
DMA_ISSUE_UNROLL = 8


def _rms(x, g):
    return x * lax.rsqrt(jnp.mean(x * x, axis=-1, keepdims=True) + EPS) * g


def _cparams(sem, **kw):
    return pltpu.CompilerParams(dimension_semantics=sem, vmem_limit_bytes=V7X_VMEM_LIMIT, **kw)


def _mem_kv_kernel(mem_ref, g_ref, w_ref, gk_ref, mk_ref, mv_ref):
    m = _rms(mem_ref[0], g_ref[...]).astype(BF16)
    kv = jnp.dot(m, w_ref[...], preferred_element_type=F32)
    for h in range(N_MEM_HEADS):
        sl = slice(h * HEAD_DIM, (h + 1) * HEAD_DIM)
        mk_ref[0, :, sl] = _rms(kv[:, sl], gk_ref[...]).astype(BF16)
    mv_ref[0] = kv[:, MEM_WIDTH:].astype(BF16)


def _mem_kv(mem, g_mem, w_mem_kv, g_mk):
    b, n_mem, d = mem.shape
    out = jax.ShapeDtypeStruct((b, n_mem, MEM_WIDTH), BF16)
    return pl.pallas_call(
        _mem_kv_kernel,
        out_shape=(out, out),
        grid=(b,),
        in_specs=[
            pl.BlockSpec((1, n_mem, d), lambda i: (i, 0, 0)),
            pl.BlockSpec((1, d), lambda i: (0, 0)),
            pl.BlockSpec((d, 2 * MEM_WIDTH), lambda i: (0, 0)),
            pl.BlockSpec((1, HEAD_DIM), lambda i: (0, 0)),
        ],
        out_specs=(pl.BlockSpec((1, n_mem, MEM_WIDTH), lambda i: (i, 0, 0)),
                   pl.BlockSpec((1, n_mem, MEM_WIDTH), lambda i: (i, 0, 0))),
        compiler_params=_cparams(("arbitrary",)),
        name="mem_kv",
    )(mem, g_mem.reshape(1, d), w_mem_kv, g_mk.reshape(1, HEAD_DIM))


def _rope_tables(seq_len):
    t = jnp.arange(seq_len, dtype=jnp.int32)
    row = (t // GRID_W).astype(F32)
    col = (t % GRID_W).astype(F32)
    inv_freq = ROPE_THETA ** (-jnp.arange(0, ROPE_AXIS_DIM, 2, dtype=F32) / ROPE_AXIS_DIM)
    ang_r = row[:, None] * inv_freq
    ang_c = col[:, None] * inv_freq
    cos = jnp.concatenate([jnp.cos(ang_r), jnp.cos(ang_r), jnp.cos(ang_c), jnp.cos(ang_c)], axis=-1)
    sin = jnp.concatenate([-jnp.sin(ang_r), jnp.sin(ang_r), -jnp.sin(ang_c), jnp.sin(ang_c)], axis=-1)
    return cos, sin


def _in_proj_kernel(x_ref, gmix_ref, w_ref, gq_ref, gk_ref, gmq_ref, cos_ref, sin_ref,
                    q_ref, k_ref, v_ref, u_ref, cb_ref, mq_ref):
    tm = x_ref.shape[0]
    hb = _rms(x_ref[...], gmix_ref[...]).astype(BF16)
    cos = cos_ref[...]
    sin = sin_ref[...]
    lane = lax.broadcasted_iota(jnp.int32, (tm, HEAD_DIM), 1)
    first_half = (lane % (ROPE_AXIS_DIM)) < (ROPE_AXIS_DIM // 2)
    scale = HEAD_DIM ** -0.5
    q_scale = scale * LOG2_E

    def rope(p):
        swapped = jnp.where(first_half, pltpu.roll(p, HEAD_DIM - ROPE_AXIS_DIM // 2, 1),
                            pltpu.roll(p, ROPE_AXIS_DIM // 2, 1))
        return p * cos + swapped * sin

    def proj(lo, width):
        return jnp.dot(hb, w_ref[:, lo:lo + width], preferred_element_type=F32)

    pq = proj(0, ATTN_WIDTH)
    for h in range(N_Q_HEADS):
        sl = slice(h * HEAD_DIM, (h + 1) * HEAD_DIM)
        q_ref[:, sl] = (rope(_rms(pq[:, sl], gq_ref[...])) * q_scale).astype(BF16)
    pkv = proj(ATTN_WIDTH, 2 * KV_WIDTH)
    for h in range(N_KV_HEADS):
        sl = slice(h * HEAD_DIM, (h + 1) * HEAD_DIM)
        k_ref[:, sl] = rope(_rms(pkv[:, sl], gk_ref[...])).astype(BF16)
    v_ref[...] = pkv[:, KV_WIDTH:].astype(BF16)
    pc = proj(ATTN_WIDTH + 2 * KV_WIDTH, 3 * CONV_WIDTH)
    u_ref[...] = (pc[:, 2 * CONV_WIDTH:] * pc[:, :CONV_WIDTH]).astype(BF16)
    cb_ref[...] = pc[:, CONV_WIDTH:2 * CONV_WIDTH].astype(BF16)
    pm = proj(ATTN_WIDTH + 2 * KV_WIDTH + 3 * CONV_WIDTH, MEM_WIDTH)
    for h in range(N_MEM_HEADS):
        sl = slice(h * HEAD_DIM, (h + 1) * HEAD_DIM)
        mq_ref[:, sl] = (_rms(pm[:, sl], gmq_ref[...]) * scale).astype(BF16)


def _in_proj(x2d, seq_len, g_mix, w_in_b, g_q, g_k, g_mq, cos, sin, tm):
    m, d = x2d.shape
    in_width = w_in_b.shape[1]
    tiles_per_seq = seq_len // tm
    row = lambda i: (i, 0)
    const = lambda i: (0, 0)
    pos = lambda i: (i % tiles_per_seq, 0)
    outs = [(ATTN_WIDTH,), (KV_WIDTH,), (KV_WIDTH,), (CONV_WIDTH,), (CONV_WIDTH,), (MEM_WIDTH,)]
    return pl.pallas_call(
        _in_proj_kernel,
        out_shape=tuple(jax.ShapeDtypeStruct((m, w[0]), BF16) for w in outs),
        grid=(m // tm,),
        in_specs=[
            pl.BlockSpec((tm, d), row),
            pl.BlockSpec((1, d), const),
            pl.BlockSpec((d, in_width), const, pipeline_mode=pl.Buffered(1)),
            pl.BlockSpec((1, HEAD_DIM), const),
            pl.BlockSpec((1, HEAD_DIM), const),
            pl.BlockSpec((1, HEAD_DIM), const),
            pl.BlockSpec((tm, HEAD_DIM), pos),
            pl.BlockSpec((tm, HEAD_DIM), pos),
        ],
        out_specs=tuple(pl.BlockSpec((tm, w[0]), row) for w in outs),
        compiler_params=_cparams(("parallel",)),
        name="in_proj",
    )(x2d, g_mix.reshape(1, d), w_in_b, g_q.reshape(1, HEAD_DIM), g_k.reshape(1, HEAD_DIM),
      g_mq.reshape(1, HEAD_DIM), cos, sin)


def _attn_kernel(q_ref, k_ref, v_ref, o_ref, s_a, s_b, acc_ref, *, tk):
    tq = q_ref.shape[1]
    seq = k_ref.shape[1]
    n = seq // tk
    rows = Q_PER_KV * tq
    q = jnp.concatenate([q_ref[0, :, g * HEAD_DIM:(g + 1) * HEAD_DIM] for g in range(Q_PER_KV)], axis=0)
    ones = jnp.ones((tk, HEAD_DIM), BF16)

    def scores(j, s_ref):
        start = pl.multiple_of(j * tk, tk)
        s = lax.dot_general(q, k_ref[0, pl.ds(start, tk), :], (((1,), (1,)), ((), ())),
                            preferred_element_type=F32)
        s_ref[...] = s
        return jnp.max(s, axis=1, keepdims=True)

    def update(j, s_ref, mx, m, l):
        start = pl.multiple_of(j * tk, tk)
        m_new = jnp.maximum(m, mx)
        alpha = jnp.exp2(m - m_new)
        p = jnp.exp2(s_ref[...] - m_new).astype(BF16)
        v1 = jnp.concatenate([v_ref[0, pl.ds(start, tk), :], ones], axis=1)
        pv = jnp.dot(p, v1, preferred_element_type=F32)
        acc_ref[...] = alpha * acc_ref[...] + pv[:, :HEAD_DIM]
        return m_new, alpha * l + pv[:, HEAD_DIM:]

    acc_ref[...] = jnp.zeros_like(acc_ref)
    m0 = jnp.full((rows, 1), NEG_BIG, F32)
    l0 = jnp.zeros((rows, HEAD_DIM), F32)
    mx0 = scores(0, s_a)

    def pair(i, carry):
        mx_a, m, l = carry
        mx_b = scores(2 * i + 1, s_b)
        m, l = update(2 * i, s_a, mx_a, m, l)
        mx_a = scores(2 * i + 2, s_a)
        m, l = update(2 * i + 1, s_b, mx_b, m, l)
        return mx_a, m, l

    mx_a, m, l = lax.fori_loop(0, n // 2 - 1, pair, (mx0, m0, l0))
    mx_b = scores(n - 1, s_b)
    m, l = update(n - 2, s_a, mx_a, m, l)
    m, l = update(n - 1, s_b, mx_b, m, l)
    o = acc_ref[...] / l
    for g in range(Q_PER_KV):
        o_ref[0, :, g * HEAD_DIM:(g + 1) * HEAD_DIM] = o[g * tq:(g + 1) * tq].astype(BF16)


def _attention(q, k, v, tq, tk):
    b, s, _ = q.shape
    assert (s // tk) % 2 == 0, "key chunks are processed in pairs"
    grp_w = Q_PER_KV * HEAD_DIM
    rows = Q_PER_KV * tq
    return pl.pallas_call(
        functools.partial(_attn_kernel, tk=tk),
        out_shape=jax.ShapeDtypeStruct((b, s, ATTN_WIDTH), BF16),
        grid=(b, N_KV_HEADS, s // tq),
        in_specs=[
            pl.BlockSpec((1, tq, grp_w), lambda bi, h, qi: (bi, qi, h)),
            pl.BlockSpec((1, s, HEAD_DIM), lambda bi, h, qi: (bi, 0, h)),
            pl.BlockSpec((1, s, HEAD_DIM), lambda bi, h, qi: (bi, 0, h)),
        ],
        out_specs=pl.BlockSpec((1, tq, grp_w), lambda bi, h, qi: (bi, qi, h)),
        scratch_shapes=[pltpu.VMEM((rows, tk), F32), pltpu.VMEM((rows, tk), F32),
                        pltpu.VMEM((rows, HEAD_DIM), F32)],
        compiler_params=_cparams(("parallel", "parallel", "parallel")),
        name="attention",
    )(q, k, v)


def _route(logits):
    tm = logits.shape[0]
    lane = lax.broadcasted_iota(jnp.int32, (tm, ROUTE_LANES), 1)
    lane_f = lane.astype(F32)
    no_lane = float(ROUTE_LANES)
    is_g = lane < N_GROUPS
    gl = jnp.where(is_g, logits, NEG_BIG)
    gmax = jnp.max(gl, axis=1, keepdims=True)
    gsum = jnp.sum(jnp.where(is_g, jnp.exp(gl - gmax), 0.0), axis=1, keepdims=True)
    p_group = 1.0 / gsum
    gidx = jnp.min(jnp.where(is_g & (gl == gmax), lane_f, no_lane), axis=1, keepdims=True)
    lo = N_GROUPS + EXPERTS_PER_GROUP * gidx
    in_grp = (lane_f >= lo) & (lane_f < lo + EXPERTS_PER_GROUP)
    el = jnp.where(in_grp, logits, NEG_BIG)
    emax = jnp.max(el, axis=1, keepdims=True)
    j1 = jnp.min(jnp.where(in_grp & (el == emax), lane_f, no_lane), axis=1, keepdims=True)
    rest = in_grp & (lane_f != j1)
    el2 = jnp.where(rest, logits, NEG_BIG)
    e2max = jnp.max(el2, axis=1, keepdims=True)
    j2 = jnp.min(jnp.where(rest & (el2 == e2max), lane_f, no_lane), axis=1, keepdims=True)
    r = jnp.exp(e2max - emax)
    g1 = p_group / (1.0 + r)
    g2 = g1 * r
    return j1 - N_GROUPS, j2 - N_GROUPS, g1, g2


def _slot_ranks(e1, e2, counts):
    tm = e1.shape[0]
    lane_f = lax.broadcasted_iota(jnp.int32, (tm, ROUTE_LANES), 1).astype(F32)
    oh1 = lane_f == e1
    oh2 = lane_f == e2
    both = jnp.where(oh1 | oh2, 1.0, 0.0)
    earlier = (lax.broadcasted_iota(jnp.int32, (tm, tm), 1)
               < lax.broadcasted_iota(jnp.int32, (tm, tm), 0))
    before = jnp.dot(jnp.where(earlier, 1.0, 0.0).astype(BF16), both.astype(BF16),
                     preferred_element_type=F32) + counts
    rank1 = jnp.sum(jnp.where(oh1, before, 0.0), axis=1, keepdims=True)
    rank2 = jnp.sum(jnp.where(oh2, before, 0.0), axis=1, keepdims=True)
    return rank1, rank2, counts + jnp.sum(both, axis=0, keepdims=True)


def _split_bf16(a):
    hi = a.astype(BF16)
    lo = (a - hi.astype(F32)).astype(BF16)
    return hi, lo


def _mix_out_kernel(ya_ref, u_ref, up_ref, un_ref, cb_ref, mq_ref, mk_ref, mv_ref, x_ref,
                    wo_ref, gout_ref, convw_ref, convb_ref, gffn_ref, wr_hi_ref, wr_lo_ref, cnt_in_ref,
                    h2_buf_ref, x1_ref, h2_ref, info_ref, cnt_out_ref, cnt_ref, *, tiles_per_seq):
    del h2_buf_ref
    tm = x_ref.shape[0]
    i = pl.program_id(0)
    t_in_seq = i % tiles_per_seq

    @pl.when(i == 0)
    def _():
        cnt_ref[...] = cnt_in_ref[...]

    a = _rms(ya_ref[...].astype(F32), gout_ref[:, :ATTN_WIDTH])

    u = u_ref[...].astype(F32)
    rowi = lax.broadcasted_iota(jnp.int32, (tm, CONV_WIDTH), 0)
    prev_row = jnp.where(t_in_seq == 0, 0.0, up_ref[7:8, :].astype(F32))
    next_row = jnp.where(t_in_seq == tiles_per_seq - 1, 0.0, un_ref[0:1, :].astype(F32))
    u_prev = jnp.where(rowi == 0, prev_row, pltpu.roll(u, 1, 0))
    u_next = jnp.where(rowi == tm - 1, next_row, pltpu.roll(u, tm - 1, 0))
    conv = u_prev * convw_ref[0:1, :] + u * convw_ref[1:2, :] + u_next * convw_ref[2:3, :] + convb_ref[...]
    c = _rms(cb_ref[...].astype(F32) * conv, gout_ref[:, ATTN_WIDTH:ATTN_WIDTH + CONV_WIDTH])

    heads = []
    for h in range(N_MEM_HEADS):
        sl = slice(h * HEAD_DIM, (h + 1) * HEAD_DIM)
        s = lax.dot_general(mq_ref[:, sl], mk_ref[0, :, sl], (((1,), (1,)), ((), ())),
                            preferred_element_type=F32)
        p = jnp.exp(s - jnp.max(s, axis=1, keepdims=True))
        o = jnp.dot(p.astype(BF16), mv_ref[0, :, sl], preferred_element_type=F32)
        heads.append(o / jnp.sum(p, axis=1, keepdims=True))
    ym = _rms(jnp.concatenate(heads, axis=1), gout_ref[:, ATTN_WIDTH + CONV_WIDTH:])

    mixed = jnp.concatenate([a, c, ym], axis=1).astype(BF16)
    x1 = x_ref[...] + jnp.dot(mixed, wo_ref[...], preferred_element_type=F32)
    x1_ref[...] = x1
    h2 = _rms(x1, gffn_ref[...])
    h2_ref[...] = h2
    h_hi, h_lo = _split_bf16(h2)
    logits = (jnp.dot(h_hi, wr_hi_ref[...], preferred_element_type=F32)
              + jnp.dot(h_hi, wr_lo_ref[...], preferred_element_type=F32)
              + jnp.dot(h_lo, wr_hi_ref[...], preferred_element_type=F32))
    e1, e2, g1, g2 = _route(logits)
    rank1, rank2, counts = _slot_ranks(e1, e2, cnt_ref[...])
    cnt_ref[...] = counts
    cnt_out_ref[...] = counts
    lane = lax.broadcasted_iota(jnp.int32, (tm, ROUTE_LANES), 1)
    info = jnp.zeros((tm, ROUTE_LANES), F32)
    for k, col in enumerate((e1, e2, g1, g2, rank1, rank2)):
        info = jnp.where(lane == k, col, info)
    info_ref[...] = info


def _mix_out(y_attn, u, cb, mq, mk, mv, x2d, seq_len, w_o_b, g_out, conv_w, conv_b, g_ffn,
             wr_hi, wr_lo, tm, counts_in, h2_row0, h2_buf):
    m, d = x2d.shape
    n_mem = mk.shape[1]
    tiles_per_seq = seq_len // tm
    sub = tm // 8
    n_sub = m // 8
    h2_tile0 = h2_row0 // tm
    row = lambda i: (i, 0)
    const = lambda i: (0, 0)
    batch = lambda i: (i // tiles_per_seq, 0, 0)
    h2_buf_arg = 17
    extra_specs, extra_args = [pl.BlockSpec(memory_space=pl.ANY)], [h2_buf]
    return pl.pallas_call(
        functools.partial(_mix_out_kernel, tiles_per_seq=tiles_per_seq),
        out_shape=(jax.ShapeDtypeStruct((m, d), F32), jax.ShapeDtypeStruct(h2_buf.shape, F32),
                   jax.ShapeDtypeStruct((m, ROUTE_LANES), F32),
                   jax.ShapeDtypeStruct((1, ROUTE_LANES), F32)),
        grid=(m // tm,),
        input_output_aliases={h2_buf_arg: 1},
        scratch_shapes=[pltpu.VMEM((1, ROUTE_LANES), F32)],
        in_specs=[
            pl.BlockSpec((tm, ATTN_WIDTH), row),
            pl.BlockSpec((tm, CONV_WIDTH), row),
            pl.BlockSpec((8, CONV_WIDTH), lambda i: (jnp.maximum(i * sub - 1, 0), 0)),
            pl.BlockSpec((8, CONV_WIDTH), lambda i: (jnp.minimum((i + 1) * sub, n_sub - 1), 0)),
            pl.BlockSpec((tm, CONV_WIDTH), row),
            pl.BlockSpec((tm, MEM_WIDTH), row),
            pl.BlockSpec((1, n_mem, MEM_WIDTH), batch),
            pl.BlockSpec((1, n_mem, MEM_WIDTH), batch),
            pl.BlockSpec((tm, d), row),
            pl.BlockSpec((MIX_WIDTH, d), const, pipeline_mode=pl.Buffered(1)),
            pl.BlockSpec((1, MIX_WIDTH), const),
            pl.BlockSpec((3, CONV_WIDTH), const),
            pl.BlockSpec((1, CONV_WIDTH), const),
            pl.BlockSpec((1, d), const),
            pl.BlockSpec((d, ROUTE_LANES), const),
            pl.BlockSpec((d, ROUTE_LANES), const),
            pl.BlockSpec((1, ROUTE_LANES), const),
        ] + extra_specs,
        out_specs=(pl.BlockSpec((tm, d), row), pl.BlockSpec((tm, d), lambda i: (i + h2_tile0, 0)),
                   pl.BlockSpec((tm, ROUTE_LANES), row), pl.BlockSpec((1, ROUTE_LANES), const)),
        compiler_params=_cparams(("arbitrary",)),
        name="mix_out",
    )(y_attn, u, u, u, cb, mq, mk, mv, x2d, w_o_b, g_out.reshape(1, MIX_WIDTH), conv_w,
      conv_b.reshape(1, CONV_WIDTH), g_ffn.reshape(1, d), wr_hi, wr_lo, counts_in, *extra_args)


def _row_copy(src_hbm, idx, dst, row, sem):
    return pltpu.make_async_copy(src_hbm.at[pl.ds(idx, 1), :], dst.at[pl.ds(row, 1), :], sem)


def _experts_kernel(bexp_ref, nused_ref, tok_ref, h2_hbm, wg_ref, wu_ref, wd_ref, ys_ref,
                    xbuf, sem, wg_b, wu_b, wd_b):
    i = pl.program_id(0)
    n_used = nused_ref[0]
    blk = DISPATCH_BLOCK

    def start_gather(b, slot):
        def body(r, carry):
            _row_copy(h2_hbm, tok_ref[b * blk + r], xbuf.at[slot], r, sem.at[slot]).start()
            return carry
        lax.fori_loop(0, blk, body, 0, unroll=DMA_ISSUE_UNROLL)

    @pl.when(i == 0)
    def _():
        start_gather(0, 0)

    @pl.when(i + 1 < n_used)
    def _():
        start_gather(i + 1, (i + 1) % 2)

    @pl.when(i < n_used)
    def _():
        slot = i % 2
        pltpu.make_async_copy(h2_hbm.at[pl.ds(0, blk), :], xbuf.at[slot], sem.at[slot]).wait()

        @pl.when((i == 0) | (bexp_ref[i] != bexp_ref[jnp.maximum(i - 1, 0)]))
        def _():
            wg_b[...] = wg_ref[...].astype(BF16)
            wu_b[...] = wu_ref[...].astype(BF16)
            wd_b[...] = wd_ref[...].astype(BF16)

        x = xbuf[slot].astype(BF16)
        g = jnp.dot(x, wg_b[...], preferred_element_type=F32)
        up = jnp.dot(x, wu_b[...], preferred_element_type=F32)
        hid = (g / (1.0 + jnp.exp(-g))) * up
        ys_ref[...] = jnp.dot(hid.astype(BF16), wd_b[...], preferred_element_type=F32)

    @pl.when(i >= n_used)
    def _():
        ys_ref[...] = jnp.zeros_like(ys_ref)


def _experts(block_expert, n_used, token_of_pos, h2, w_gate, w_up, w_down):
    n_blocks = block_expert.shape[0]
    d = h2.shape[1]
    blk = DISPATCH_BLOCK
    wmap = lambda i, bexp, nused, tok: (bexp[i], 0, 0)
    grid_spec = pltpu.PrefetchScalarGridSpec(
        num_scalar_prefetch=3,
        grid=(n_blocks,),
        in_specs=[
            pl.BlockSpec(memory_space=pl.ANY),
            pl.BlockSpec((None, d, D_FF_EXPERT), wmap),
            pl.BlockSpec((None, d, D_FF_EXPERT), wmap),
            pl.BlockSpec((None, D_FF_EXPERT, d), wmap),
        ],
        out_specs=pl.BlockSpec((blk, d), lambda i, bexp, nused, tok: (i, 0)),
        scratch_shapes=[
            pltpu.VMEM((2, blk, d), F32),
            pltpu.SemaphoreType.DMA((2,)),
            pltpu.VMEM((d, D_FF_EXPERT), BF16),
            pltpu.VMEM((d, D_FF_EXPERT), BF16),
            pltpu.VMEM((D_FF_EXPERT, d), BF16),
        ],
    )
    return pl.pallas_call(
        _experts_kernel,
        out_shape=jax.ShapeDtypeStruct((n_blocks * blk, d), F32),
        grid_spec=grid_spec,
        compiler_params=_cparams(("arbitrary",)),
        name="experts",
    )(block_expert, n_used, token_of_pos, h2, w_gate, w_up, w_down)


def _combine_kernel(pos_ref, x1_ref, info_ref, ys_hbm, out_ref, ybuf, sem):
    i = pl.program_id(0)
    n = pl.num_programs(0)
    tm = x1_ref.shape[0]

    def start_gather(t, slot):
        def body(r, carry):
            base = (t * tm + r) * TOP_K
            for k in range(TOP_K):
                _row_copy(ys_hbm, pos_ref[base + k], ybuf.at[slot, k], r, sem.at[slot]).start()
            return carry
        lax.fori_loop(0, tm, body, 0, unroll=DMA_ISSUE_UNROLL // TOP_K)

    @pl.when(i == 0)
    def _():
        start_gather(0, 0)

    @pl.when(i + 1 < n)
    def _():
        start_gather(i + 1, (i + 1) % 2)

    slot = i % 2
    for k in range(TOP_K):
        pltpu.make_async_copy(ys_hbm.at[pl.ds(0, tm), :], ybuf.at[slot, k], sem.at[slot]).wait()
    info = info_ref[...]
    out_ref[...] = (x1_ref[...] + ybuf[slot, 0] * info[:, 2:3] + ybuf[slot, 1] * info[:, 3:4])


def _combine(pos, x1, info, ys, tm):
    m, d = x1.shape
    grid_spec = pltpu.PrefetchScalarGridSpec(
        num_scalar_prefetch=1,
        grid=(m // tm,),
        in_specs=[
            pl.BlockSpec((tm, d), lambda i, pos: (i, 0)),
            pl.BlockSpec((tm, ROUTE_LANES), lambda i, pos: (i, 0)),
            pl.BlockSpec(memory_space=pl.ANY),
        ],
        out_specs=pl.BlockSpec((tm, d), lambda i, pos: (i, 0)),
        scratch_shapes=[pltpu.VMEM((2, TOP_K, tm, d), F32), pltpu.SemaphoreType.DMA((2,))],
    )
    return pl.pallas_call(
        _combine_kernel,
        out_shape=jax.ShapeDtypeStruct((m, d), F32),
        grid_spec=grid_spec,
        compiler_params=_cparams(("arbitrary",)),
        name="combine",
    )(pos, x1, info, ys)


def _dispatch_plan(experts_flat, rank_flat, counts):
    n_slots = experts_flat.shape[0]
    blk = DISPATCH_BLOCK
    n_blocks = -(-n_slots // blk) + N_EXPERTS
    rank = rank_flat
    padded = ((counts + blk - 1) // blk) * blk
    padded_end = jnp.cumsum(padded)
    padded_start = padded_end - padded
    pos = (padded_start[experts_flat] + rank).astype(jnp.int32)
    token_of_pos = jnp.zeros((n_blocks * blk,), jnp.int32).at[pos].set(
        jnp.arange(n_slots, dtype=jnp.int32) // TOP_K)
    n_used = (padded_end[-1] // blk).astype(jnp.int32)
    block_start = jnp.arange(n_blocks, dtype=jnp.int32) * blk
    last_start = jnp.maximum(padded_end[-1] - blk, 0)
    block_expert = jnp.searchsorted(padded_end, jnp.minimum(block_start, last_start), side='right')
    block_expert = jnp.minimum(block_expert, N_EXPERTS - 1).astype(jnp.int32)
    return pos, token_of_pos, block_expert, n_used.reshape(1)


def _tile(n, pref):
    return pref if n % pref == 0 else n


def kernel(x_prompt, x_sample, mem_prompt, mem_sample, g_mix, w_in, g_q, g_k, conv_w, conv_b, g_mem, w_mem_kv, g_mq, g_mk, g_out, w_o, g_ffn, w_route_group, w_route_expert, w_gate, w_up, w_down):
    assert g_mix.shape[0] == 1, "single layer"
    d = D_MODEL
    w_in_b = w_in[0].astype(BF16)
    w_o_b = w_o[0].astype(BF16)
    w_mem_b = w_mem_kv[0].astype(BF16)
    w_route = jnp.concatenate(
        [w_route_group[0], w_route_expert[0],
         jnp.zeros((d, ROUTE_LANES - N_GROUPS - N_EXPERTS), F32)], axis=1)
    wr_hi, wr_lo = _split_bf16(w_route)

    groups = []
    m_total = x_prompt.shape[0] * x_prompt.shape[1] + x_sample.shape[0] * x_sample.shape[1]
    counts = jnp.zeros((1, ROUTE_LANES), F32)
    h2_all = jnp.zeros((m_total, d), F32)
    row0 = 0
    for x, mem in ((x_prompt, mem_prompt), (x_sample, mem_sample)):
        b, s, _ = x.shape
        tm = _tile(s, 256)
        x2d = x.reshape(b * s, d)
        cos, sin = _rope_tables(s)
        mk, mv = _mem_kv(mem, g_mem[0], w_mem_b, g_mk[0])
        q, k, v, u, cb, mq = _in_proj(x2d, s, g_mix[0], w_in_b, g_q[0], g_k[0], g_mq[0], cos, sin, tm)
        y_attn = _attention(q.reshape(b, s, ATTN_WIDTH), k.reshape(b, s, KV_WIDTH),
                            v.reshape(b, s, KV_WIDTH), _tile(s, 128), 512 if s % 1024 == 0 else s // 2)
        x1, h2_all, info, counts = _mix_out(
            y_attn.reshape(b * s, ATTN_WIDTH), u, cb, mq, mk, mv, x2d, s, w_o_b, g_out[0], conv_w[0],
            conv_b[0], g_ffn[0], wr_hi, wr_lo, tm, counts, row0, h2_all)
        groups.append((x.shape, x1, info, tm))
        row0 += b * s

    route = jnp.concatenate([g[2][:, :6] for g in groups], axis=0)
    experts_flat = route[:, 0:TOP_K].astype(jnp.int32).reshape(-1)
    rank_flat = route[:, 4:4 + TOP_K].astype(jnp.int32).reshape(-1)
    pos, token_of_pos, block_expert, n_used = _dispatch_plan(
        experts_flat, rank_flat, counts[0, :N_EXPERTS].astype(jnp.int32))
    ys = _experts(block_expert, n_used, token_of_pos, h2_all, w_gate[0], w_up[0], w_down[0])

    outs = []
    slot0 = 0
    for shape, x1, info, tm in groups:
        m = x1.shape[0]
        out = _combine(pos[slot0:slot0 + m * TOP_K], x1, info, ys, tm)
        outs.append(out.reshape(shape))
        slot0 += m * TOP_K
    return tuple(outs)
```

```python
import functools

import jax
import jax.numpy as jnp
from jax import lax
from jax.experimental import pallas as pl
from jax.experimental.pallas import tpu as pltpu

D_MODEL = 2048
HEAD_DIM = 128
N_Q_HEADS = 8
N_KV_HEADS = 2
Q_PER_KV = N_Q_HEADS // N_KV_HEADS
ATTN_WIDTH = N_Q_HEADS * HEAD_DIM
KV_WIDTH = N_KV_HEADS * HEAD_DIM
CONV_WIDTH = 512
N_MEM_HEADS = 4
MEM_WIDTH = N_MEM_HEADS * HEAD_DIM
MIX_WIDTH = ATTN_WIDTH + CONV_WIDTH + MEM_WIDTH
GRID_W = 64
ROPE_AXIS_DIM = HEAD_DIM // 2
ROPE_THETA = 10000.0
N_GROUPS = 8
EXPERTS_PER_GROUP = 8
N_EXPERTS = N_GROUPS * EXPERTS_PER_GROUP
TOP_K = 2
D_FF_EXPERT = 512
DISPATCH_BLOCK = 256
EPS = 1e-6

ROUTE_LANES = 128
V7X_VMEM_LIMIT = 56 * 1024 * 1024

F32 = jnp.float32
BF16 = jnp.bfloat16
NEG_BIG = -1e30
LOG2_E = 1.4426950408889634
DMA_ISSUE_UNROLL = 8


def _rms(x, g):
    return x * lax.rsqrt(jnp.mean(x * x, axis=-1, keepdims=True) + EPS) * g


def _cparams(sem, **kw):
    return pltpu.CompilerParams(dimension_semantics=sem, vmem_limit_bytes=V7X_VMEM_LIMIT, **kw)


def _mem_kv_kernel(mem_ref, g_ref, w_ref, gk_ref, mk_ref, mv_ref):
    m = _rms(mem_ref[0], g_ref[...]).astype(BF16)
    kv = jnp.dot(m, w_ref[...], preferred_element_type=F32)
    for h in range(N_MEM_HEADS):
        sl = slice(h * HEAD_DIM, (h + 1) * HEAD_DIM)
        mk_ref[0, :, sl] = _rms(kv[:, sl], gk_ref[...]).astype(BF16)
    mv_ref[0] = kv[:, MEM_WIDTH:].astype(BF16)


def _mem_kv(mem, g_mem, w_mem_kv, g_mk):
    b, n_mem, d = mem.shape
    out = jax.ShapeDtypeStruct((b, n_mem, MEM_WIDTH), BF16)
    return pl.pallas_call(
        _mem_kv_kernel,
        out_shape=(out, out),
        grid=(b,),
        in_specs=[
            pl.BlockSpec((1, n_mem, d), lambda i: (i, 0, 0)),
            pl.BlockSpec((1, d), lambda i: (0, 0)),
            pl.BlockSpec((d, 2 * MEM_WIDTH), lambda i: (0, 0)),
            pl.BlockSpec((1, HEAD_DIM), lambda i: (0, 0)),
        ],
        out_specs=(pl.BlockSpec((1, n_mem, MEM_WIDTH), lambda i: (i, 0, 0)),
                   pl.BlockSpec((1, n_mem, MEM_WIDTH), lambda i: (i, 0, 0))),
        compiler_params=_cparams(("arbitrary",)),
        name="mem_kv",
    )(mem, g_mem.reshape(1, d), w_mem_kv, g_mk.reshape(1, HEAD_DIM))


def _rope_tables(seq_len):
    t = jnp.arange(seq_len, dtype=jnp.int32)
    row = (t // GRID_W).astype(F32)
    col = (t % GRID_W).astype(F32)
    inv_freq = ROPE_THETA ** (-jnp.arange(0, ROPE_AXIS_DIM, 2, dtype=F32) / ROPE_AXIS_DIM)
    ang_r = row[:, None] * inv_freq
    ang_c = col[:, None] * inv_freq
    cos = jnp.concatenate([jnp.cos(ang_r), jnp.cos(ang_r), jnp.cos(ang_c), jnp.cos(ang_c)], axis=-1)
    sin = jnp.concatenate([-jnp.sin(ang_r), jnp.sin(ang_r), -jnp.sin(ang_c), jnp.sin(ang_c)], axis=-1)
    return cos, sin


def _in_proj_kernel(x_ref, gmix_ref, w_ref, gq_ref, gk_ref, gmq_ref, cos_ref, sin_ref,
                    q_ref, k_ref, v_ref, u_ref, cb_ref, mq_ref):
    tm = x_ref.shape[0]
    hb = _rms(x_ref[...], gmix_ref[...]).astype(BF16)
    cos = cos_ref[...]
    sin = sin_ref[...]
    lane = lax.broadcasted_iota(jnp.int32, (tm, HEAD_DIM), 1)
    first_half = (lane % (ROPE_AXIS_DIM)) < (ROPE_AXIS_DIM // 2)
    scale = HEAD_DIM ** -0.5
    q_scale = scale * LOG2_E

    def rope(p):
        swapped = jnp.where(first_half, pltpu.roll(p, HEAD_DIM - ROPE_AXIS_DIM // 2, 1),
                            pltpu.roll(p, ROPE_AXIS_DIM // 2, 1))
        return p * cos + swapped * sin

    def proj(lo, width):
        return jnp.dot(hb, w_ref[:, lo:lo + width], preferred_element_type=F32)

    pq = proj(0, ATTN_WIDTH)
    for h in range(N_Q_HEADS):
        sl = slice(h * HEAD_DIM, (h + 1) * HEAD_DIM)
        q_ref[:, sl] = (rope(_rms(pq[:, sl], gq_ref[...])) * q_scale).astype(BF16)
    pkv = proj(ATTN_WIDTH, 2 * KV_WIDTH)
    for h in range(N_KV_HEADS):
        sl = slice(h * HEAD_DIM, (h + 1) * HEAD_DIM)
        k_ref[:, sl] = rope(_rms(pkv[:, sl], gk_ref[...])).astype(BF16)
    v_ref[...] = pkv[:, KV_WIDTH:].astype(BF16)
    pc = proj(ATTN_WIDTH + 2 * KV_WIDTH, 3 * CONV_WIDTH)
    u_ref[...] = (pc[:, 2 * CONV_WIDTH:] * pc[:, :CONV_WIDTH]).astype(BF16)
    cb_ref[...] = pc[:, CONV_WIDTH:2 * CONV_WIDTH].astype(BF16)
    pm = proj(ATTN_WIDTH + 2 * KV_WIDTH + 3 * CONV_WIDTH, MEM_WIDTH)
    for h in range(N_MEM_HEADS):
        sl = slice(h * HEAD_DIM, (h + 1) * HEAD_DIM)
        mq_ref[:, sl] = (_rms(pm[:, sl], gmq_ref[...]) * scale).astype(BF16)


def _in_proj(x2d, seq_len, g_mix, w_in_b, g_q, g_k, g_mq, cos, sin, tm):
    m, d = x2d.shape
    in_width = w_in_b.shape[1]
    tiles_per_seq = seq_len // tm
    row = lambda i: (i, 0)
    const = lambda i: (0, 0)
    pos = lambda i: (i % tiles_per_seq, 0)
    outs = [(ATTN_WIDTH,), (KV_WIDTH,), (KV_WIDTH,), (CONV_WIDTH,), (CONV_WIDTH,), (MEM_WIDTH,)]
    return pl.pallas_call(
        _in_proj_kernel,
        out_shape=tuple(jax.ShapeDtypeStruct((m, w[0]), BF16) for w in outs),
        grid=(m // tm,),
        in_specs=[
            pl.BlockSpec((tm, d), row),
            pl.BlockSpec((1, d), const),
            pl.BlockSpec((d, in_width), const, pipeline_mode=pl.Buffered(1)),
            pl.BlockSpec((1, HEAD_DIM), const),
            pl.BlockSpec((1, HEAD_DIM), const),
            pl.BlockSpec((1, HEAD_DIM), const),
            pl.BlockSpec((tm, HEAD_DIM), pos),
            pl.BlockSpec((tm, HEAD_DIM), pos),
        ],
        out_specs=tuple(pl.BlockSpec((tm, w[0]), row) for w in outs),
        compiler_params=_cparams(("parallel",)),
        name="in_proj",
    )(x2d, g_mix.reshape(1, d), w_in_b, g_q.reshape(1, HEAD_DIM), g_k.reshape(1, HEAD_DIM),
      g_mq.reshape(1, HEAD_DIM), cos, sin)


def _attn_kernel(q_ref, k_ref, v_ref, o_ref, s_a, s_b, acc_ref, *, tk):
    tq = q_ref.shape[1]
    seq = k_ref.shape[1]
    n = seq // tk
    rows = Q_PER_KV * tq
    q = jnp.concatenate([q_ref[0, :, g * HEAD_DIM:(g + 1) * HEAD_DIM] for g in range(Q_PER_KV)], axis=0)
    ones = jnp.ones((tk, HEAD_DIM), BF16)

    def scores(j, s_ref):
        s = lax.dot_general(q, k_ref[0, j * tk:(j + 1) * tk, :], (((1,), (1,)), ((), ())),
                            preferred_element_type=F32)
        s_ref[...] = s
        return jnp.max(s, axis=1, keepdims=True)

    def update(j, s_ref, mx, m, l):
        m_new = jnp.maximum(m, mx)
        alpha = jnp.exp2(m - m_new)
        p = jnp.exp2(s_ref[...] - m_new).astype(BF16)
        v1 = jnp.concatenate([v_ref[0, j * tk:(j + 1) * tk, :], ones], axis=1)
        pv = jnp.dot(p, v1, preferred_element_type=F32)
        acc_ref[...] = alpha * acc_ref[...] + pv[:, :HEAD_DIM]
        return m_new, alpha * l + pv[:, HEAD_DIM:]

    acc_ref[...] = jnp.zeros_like(acc_ref)
    m0 = jnp.full((rows, 1), NEG_BIG, F32)
    l0 = jnp.zeros((rows, HEAD_DIM), F32)
    bufs = (s_a, s_b)
    m, l = m0, l0
    mx = scores(0, bufs[0])
    for j in range(n):
        mx_next = scores(j + 1, bufs[(j + 1) % 2]) if j + 1 < n else None
        m, l = update(j, bufs[j % 2], mx, m, l)
        mx = mx_next
    o = acc_ref[...] / l
    for g in range(Q_PER_KV):
        o_ref[0, :, g * HEAD_DIM:(g + 1) * HEAD_DIM] = o[g * tq:(g + 1) * tq].astype(BF16)


def _attention(q, k, v, tq, tk):
    b, s, _ = q.shape
    grp_w = Q_PER_KV * HEAD_DIM
    rows = Q_PER_KV * tq
    return pl.pallas_call(
        functools.partial(_attn_kernel, tk=tk),
        out_shape=jax.ShapeDtypeStruct((b, s, ATTN_WIDTH), BF16),
        grid=(b, N_KV_HEADS, s // tq),
        in_specs=[
            pl.BlockSpec((1, tq, grp_w), lambda bi, h, qi: (bi, qi, h)),
            pl.BlockSpec((1, s, HEAD_DIM), lambda bi, h, qi: (bi, 0, h)),
            pl.BlockSpec((1, s, HEAD_DIM), lambda bi, h, qi: (bi, 0, h)),
        ],
        out_specs=pl.BlockSpec((1, tq, grp_w), lambda bi, h, qi: (bi, qi, h)),
        scratch_shapes=[pltpu.VMEM((rows, tk), F32), pltpu.VMEM((rows, tk), F32),
                        pltpu.VMEM((rows, HEAD_DIM), F32)],
        compiler_params=_cparams(("parallel", "parallel", "parallel")),
        name="attention",
    )(q, k, v)


def _route(logits):
    tm = logits.shape[0]
    lane = lax.broadcasted_iota(jnp.int32, (tm, ROUTE_LANES), 1)
    lane_f = lane.astype(F32)
    no_lane = float(ROUTE_LANES)
    is_g = lane < N_GROUPS
    gl = jnp.where(is_g, logits, NEG_BIG)
    gmax = jnp.max(gl, axis=1, keepdims=True)
    gsum = jnp.sum(jnp.where(is_g, jnp.exp(gl - gmax), 0.0), axis=1, keepdims=True)
    p_group = 1.0 / gsum
    gidx = jnp.min(jnp.where(is_g & (gl == gmax), lane_f, no_lane), axis=1, keepdims=True)
    lo = N_GROUPS + EXPERTS_PER_GROUP * gidx
    in_grp = (lane_f >= lo) & (lane_f < lo + EXPERTS_PER_GROUP)
    el = jnp.where(in_grp, logits, NEG_BIG)
    emax = jnp.max(el, axis=1, keepdims=True)
    j1 = jnp.min(jnp.where(in_grp & (el == emax), lane_f, no_lane), axis=1, keepdims=True)
    rest = in_grp & (lane_f != j1)
    el2 = jnp.where(rest, logits, NEG_BIG)
    e2max = jnp.max(el2, axis=1, keepdims=True)
    j2 = jnp.min(jnp.where(rest & (el2 == e2max), lane_f, no_lane), axis=1, keepdims=True)
    r = jnp.exp(e2max - emax)
    g1 = p_group / (1.0 + r)
    g2 = g1 * r
    return j1 - N_GROUPS, j2 - N_GROUPS, g1, g2


def _slot_ranks(e1, e2, counts):
    tm = e1.shape[0]
    lane_f = lax.broadcasted_iota(jnp.int32, (tm, ROUTE_LANES), 1).astype(F32)
    oh1 = lane_f == e1
    oh2 = lane_f == e2
    both = jnp.where(oh1 | oh2, 1.0, 0.0)
    earlier = (lax.broadcasted_iota(jnp.int32, (tm, tm), 1)
               < lax.broadcasted_iota(jnp.int32, (tm, tm), 0))
    before = jnp.dot(jnp.where(earlier, 1.0, 0.0).astype(BF16), both.astype(BF16),
                     preferred_element_type=F32) + counts
    rank1 = jnp.sum(jnp.where(oh1, before, 0.0), axis=1, keepdims=True)
    rank2 = jnp.sum(jnp.where(oh2, before, 0.0), axis=1, keepdims=True)
    return rank1, rank2, counts + jnp.sum(both, axis=0, keepdims=True)


def _split_bf16(a):
    hi = a.astype(BF16)
    lo = (a - hi.astype(F32)).astype(BF16)
    return hi, lo


def _mix_out_kernel(ya_ref, u_ref, up_ref, un_ref, cb_ref, mq_ref, mk_ref, mv_ref, x_ref,
                    wo_ref, gout_ref, convw_ref, convb_ref, gffn_ref, wr_hi_ref, wr_lo_ref, cnt_in_ref,
                    h2_buf_ref, x1_ref, h2_ref, info_ref, cnt_out_ref, cnt_ref, *, tiles_per_seq):
    del h2_buf_ref
    tm = x_ref.shape[0]
    i = pl.program_id(0)
    t_in_seq = i % tiles_per_seq

    @pl.when(i == 0)
    def _():
        cnt_ref[...] = cnt_in_ref[...]

    a = _rms(ya_ref[...].astype(F32), gout_ref[:, :ATTN_WIDTH])

    u = u_ref[...].astype(F32)
    rowi = lax.broadcasted_iota(jnp.int32, (tm, CONV_WIDTH), 0)
    prev_row = jnp.where(t_in_seq == 0, 0.0, up_ref[7:8, :].astype(F32))
    next_row = jnp.where(t_in_seq == tiles_per_seq - 1, 0.0, un_ref[0:1, :].astype(F32))
    u_prev = jnp.where(rowi == 0, prev_row, pltpu.roll(u, 1, 0))
    u_next = jnp.where(rowi == tm - 1, next_row, pltpu.roll(u, tm - 1, 0))
    conv = u_prev * convw_ref[0:1, :] + u * convw_ref[1:2, :] + u_next * convw_ref[2:3, :] + convb_ref[...]
    c = _rms(cb_ref[...].astype(F32) * conv, gout_ref[:, ATTN_WIDTH:ATTN_WIDTH + CONV_WIDTH])

    heads = []
    for h in range(N_MEM_HEADS):
        sl = slice(h * HEAD_DIM, (h + 1) * HEAD_DIM)
        s = lax.dot_general(mq_ref[:, sl], mk_ref[0, :, sl], (((1,), (1,)), ((), ())),
                            preferred_element_type=F32)
        p = jnp.exp(s - jnp.max(s, axis=1, keepdims=True))
        o = jnp.dot(p.astype(BF16), mv_ref[0, :, sl], preferred_element_type=F32)
        heads.append(o / jnp.sum(p, axis=1, keepdims=True))
    ym = _rms(jnp.concatenate(heads, axis=1), gout_ref[:, ATTN_WIDTH + CONV_WIDTH:])

    mixed = jnp.concatenate([a, c, ym], axis=1).astype(BF16)
    x1 = x_ref[...] + jnp.dot(mixed, wo_ref[...], preferred_element_type=F32)
    x1_ref[...] = x1
    h2 = _rms(x1, gffn_ref[...])
    h2_ref[...] = h2
    h_hi, h_lo = _split_bf16(h2)
    logits = (jnp.dot(h_hi, wr_hi_ref[...], preferred_element_type=F32)
              + jnp.dot(h_hi, wr_lo_ref[...], preferred_element_type=F32)
              + jnp.dot(h_lo, wr_hi_ref[...], preferred_element_type=F32))
    e1, e2, g1, g2 = _route(logits)
    rank1, rank2, counts = _slot_ranks(e1, e2, cnt_ref[...])
    cnt_ref[...] = counts
    cnt_out_ref[...] = counts
    lane = lax.broadcasted_iota(jnp.int32, (tm, ROUTE_LANES), 1)
    info = jnp.zeros((tm, ROUTE_LANES), F32)
    for k, col in enumerate((e1, e2, g1, g2, rank1, rank2)):
        info = jnp.where(lane == k, col, info)
    info_ref[...] = info


def _mix_out(y_attn, u, cb, mq, mk, mv, x2d, seq_len, w_o_b, g_out, conv_w, conv_b, g_ffn,
             wr_hi, wr_lo, tm, counts_in, h2_row0, h2_buf):
    m, d = x2d.shape
    n_mem = mk.shape[1]
    tiles_per_seq = seq_len // tm
    sub = tm // 8
    n_sub = m // 8
    h2_tile0 = h2_row0 // tm
    row = lambda i: (i, 0)
    const = lambda i: (0, 0)
    batch = lambda i: (i // tiles_per_seq, 0, 0)
    h2_buf_arg = 17
    extra_specs, extra_args = [pl.BlockSpec(memory_space=pl.ANY)], [h2_buf]
    return pl.pallas_call(
        functools.partial(_mix_out_kernel, tiles_per_seq=tiles_per_seq),
        out_shape=(jax.ShapeDtypeStruct((m, d), F32), jax.ShapeDtypeStruct(h2_buf.shape, F32),
                   jax.ShapeDtypeStruct((m, ROUTE_LANES), F32),
                   jax.ShapeDtypeStruct((1, ROUTE_LANES), F32)),
        grid=(m // tm,),
        input_output_aliases={h2_buf_arg: 1},
        scratch_shapes=[pltpu.VMEM((1, ROUTE_LANES), F32)],
        in_specs=[
            pl.BlockSpec((tm, ATTN_WIDTH), row),
            pl.BlockSpec((tm, CONV_WIDTH), row),
            pl.BlockSpec((8, CONV_WIDTH), lambda i: (jnp.maximum(i * sub - 1, 0), 0)),
            pl.BlockSpec((8, CONV_WIDTH), lambda i: (jnp.minimum((i + 1) * sub, n_sub - 1), 0)),
            pl.BlockSpec((tm, CONV_WIDTH), row),
            pl.BlockSpec((tm, MEM_WIDTH), row),
            pl.BlockSpec((1, n_mem, MEM_WIDTH), batch),
            pl.BlockSpec((1, n_mem, MEM_WIDTH), batch),
            pl.BlockSpec((tm, d), row),
            pl.BlockSpec((MIX_WIDTH, d), const, pipeline_mode=pl.Buffered(1)),
            pl.BlockSpec((1, MIX_WIDTH), const),
            pl.BlockSpec((3, CONV_WIDTH), const),
            pl.BlockSpec((1, CONV_WIDTH), const),
            pl.BlockSpec((1, d), const),
            pl.BlockSpec((d, ROUTE_LANES), const),
            pl.BlockSpec((d, ROUTE_LANES), const),
            pl.BlockSpec((1, ROUTE_LANES), const),
        ] + extra_specs,
        out_specs=(pl.BlockSpec((tm, d), row), pl.BlockSpec((tm, d), lambda i: (i + h2_tile0, 0)),
                   pl.BlockSpec((tm, ROUTE_LANES), row), pl.BlockSpec((1, ROUTE_LANES), const)),
        compiler_params=_cparams(("arbitrary",)),
        name="mix_out",
    )(y_attn, u, u, u, cb, mq, mk, mv, x2d, w_o_b, g_out.reshape(1, MIX_WIDTH), conv_w,
      conv_b.reshape(1, CONV_WIDTH), g_ffn.reshape(1, d), wr_hi, wr_lo, counts_in, *extra_args)


def _row_copy(src_hbm, idx, dst, row, sem):
    return pltpu.make_async_copy(src_hbm.at[pl.ds(idx, 1), :], dst.at[pl.ds(row, 1), :], sem)


def _experts_kernel(bexp_ref, bfirst_ref, nused_ref, tok_ref, h2_hbm, wg_ref, wu_ref, wd_ref, ys_ref,
                    xbuf, sem, x_b, wg_b, wu_b, wd_b):
    i = pl.program_id(0)
    n_used = nused_ref[0]
    blk = DISPATCH_BLOCK

    def start_gather(b, slot):
        first = bfirst_ref[b]

        def body(r, carry):
            _row_copy(h2_hbm, tok_ref[first + r], xbuf.at[slot], r, sem.at[slot]).start()
            return carry
        lax.fori_loop(0, blk, body, 0, unroll=DMA_ISSUE_UNROLL)

    def wait_gather(slot):
        pltpu.make_async_copy(h2_hbm.at[pl.ds(0, blk), :], xbuf.at[slot], sem.at[slot]).wait()

    @pl.when(i == 0)
    def _():
        start_gather(0, 0)

    @pl.when(i < n_used)
    def _():
        slot = i % 2
        wait_gather(slot)

        @pl.when((i == 0) | (bexp_ref[i] != bexp_ref[jnp.maximum(i - 1, 0)]))
        def _():
            wg_b[...] = wg_ref[...].astype(BF16)
            wu_b[...] = wu_ref[...].astype(BF16)
            wd_b[...] = wd_ref[...].astype(BF16)

        x_b[...] = xbuf[slot].astype(BF16)
        nxt = bfirst_ref[jnp.minimum(i + 1, n_used - 1)]
        for r in range(blk):
            _row_copy(h2_hbm, tok_ref[nxt + r], xbuf.at[1 - slot], r, sem.at[1 - slot]).start()

        x = x_b[...]
        g = jnp.dot(x, wg_b[...], preferred_element_type=F32)
        up = jnp.dot(x, wu_b[...], preferred_element_type=F32)
        hid = (g / (1.0 + jnp.exp(-g))) * up
        ys_ref[...] = jnp.dot(hid.astype(BF16), wd_b[...], preferred_element_type=F32)

        @pl.when(i == n_used - 1)
        def _():
            wait_gather(1 - slot)

    @pl.when(i >= n_used)
    def _():
        ys_ref[...] = jnp.zeros_like(ys_ref)


def _experts(block_expert, block_first, n_used, sorted_tokens, h2, w_gate, w_up, w_down):
    n_blocks = block_expert.shape[0]
    d = h2.shape[1]
    blk = DISPATCH_BLOCK
    wmap = lambda i, bexp, bfirst, nused, tok: (bexp[i], 0, 0)
    grid_spec = pltpu.PrefetchScalarGridSpec(
        num_scalar_prefetch=4,
        grid=(n_blocks,),
        in_specs=[
            pl.BlockSpec(memory_space=pl.ANY),
            pl.BlockSpec((None, d, D_FF_EXPERT), wmap),
            pl.BlockSpec((None, d, D_FF_EXPERT), wmap),
            pl.BlockSpec((None, D_FF_EXPERT, d), wmap),
        ],
        out_specs=pl.BlockSpec((blk, d), lambda i, bexp, bfirst, nused, tok: (i, 0)),
        scratch_shapes=[
            pltpu.VMEM((2, blk, d), F32),
            pltpu.SemaphoreType.DMA((2,)),
            pltpu.VMEM((blk, d), BF16),
            pltpu.VMEM((d, D_FF_EXPERT), BF16),
            pltpu.VMEM((d, D_FF_EXPERT), BF16),
            pltpu.VMEM((D_FF_EXPERT, d), BF16),
        ],
    )
    return pl.pallas_call(
        _experts_kernel,
        out_shape=jax.ShapeDtypeStruct((n_blocks * blk, d), F32),
        grid_spec=grid_spec,
        compiler_params=_cparams(("arbitrary",)),
        name="experts",
    )(block_expert, block_first, n_used, sorted_tokens, h2, w_gate, w_up, w_down)


def _combine_kernel(pos_ref, x1_ref, info_ref, ys_hbm, out_ref, ybuf, sem):
    i = pl.program_id(0)
    n = pl.num_programs(0)
    tm = x1_ref.shape[0]

    def start_gather(t, slot):
        def body(r, carry):
            base = (t * tm + r) * TOP_K
            for k in range(TOP_K):
                _row_copy(ys_hbm, pos_ref[base + k], ybuf.at[slot, k], r, sem.at[slot]).start()
            return carry
        lax.fori_loop(0, tm, body, 0, unroll=DMA_ISSUE_UNROLL // TOP_K)

    @pl.when(i == 0)
    def _():
        start_gather(0, 0)

    @pl.when(i + 1 < n)
    def _():
        start_gather(i + 1, (i + 1) % 2)

    slot = i % 2
    for k in range(TOP_K):
        pltpu.make_async_copy(ys_hbm.at[pl.ds(0, tm), :], ybuf.at[slot, k], sem.at[slot]).wait()
    info = info_ref[...]
    out_ref[...] = (x1_ref[...] + ybuf[slot, 0] * info[:, 2:3] + ybuf[slot, 1] * info[:, 3:4])


def _combine(pos, x1, info, ys, tm):
    m, d = x1.shape
    grid_spec = pltpu.PrefetchScalarGridSpec(
        num_scalar_prefetch=1,
        grid=(m // tm,),
        in_specs=[
            pl.BlockSpec((tm, d), lambda i, pos: (i, 0)),
            pl.BlockSpec((tm, ROUTE_LANES), lambda i, pos: (i, 0)),
            pl.BlockSpec(memory_space=pl.ANY),
        ],
        out_specs=pl.BlockSpec((tm, d), lambda i, pos: (i, 0)),
        scratch_shapes=[pltpu.VMEM((2, TOP_K, tm, d), F32), pltpu.SemaphoreType.DMA((2,))],
    )
    return pl.pallas_call(
        _combine_kernel,
        out_shape=jax.ShapeDtypeStruct((m, d), F32),
        grid_spec=grid_spec,
        compiler_params=_cparams(("arbitrary",)),
        name="combine",
    )(pos, x1, info, ys)


def _dispatch_plan(experts_flat, rank_flat, counts):
    n_slots = experts_flat.shape[0]
    blk = DISPATCH_BLOCK
    n_blocks = -(-n_slots // blk) + N_EXPERTS
    rank = rank_flat
    padded = ((counts + blk - 1) // blk) * blk
    padded_end = jnp.cumsum(padded)
    padded_start = padded_end - padded
    pos = (padded_start[experts_flat] + rank).astype(jnp.int32)
    sorted_tokens = jnp.concatenate([jnp.argsort(pos).astype(jnp.int32) // TOP_K,
                                     jnp.zeros((blk,), jnp.int32)])
    n_used = (padded_end[-1] // blk).astype(jnp.int32)
    block_start = jnp.minimum(jnp.arange(n_blocks, dtype=jnp.int32) * blk,
                              jnp.maximum(padded_end[-1] - blk, 0))
    block_expert = jnp.sum(padded_end[None, :] <= block_start[:, None], axis=1)
    block_expert = jnp.minimum(block_expert, N_EXPERTS - 1).astype(jnp.int32)
    start = jnp.cumsum(counts) - counts
    block_first = (block_start - (padded_start - start)[block_expert]).astype(jnp.int32)
    return pos, sorted_tokens, block_expert, block_first, n_used.reshape(1)


def _tile(n, pref):
    return pref if n % pref == 0 else n


def kernel(x_prompt, x_sample, mem_prompt, mem_sample, g_mix, w_in, g_q, g_k, conv_w, conv_b, g_mem, w_mem_kv, g_mq, g_mk, g_out, w_o, g_ffn, w_route_group, w_route_expert, w_gate, w_up, w_down):
    assert g_mix.shape[0] == 1, "single layer"
    d = D_MODEL
    w_in_b = w_in[0].astype(BF16)
    w_o_b = w_o[0].astype(BF16)
    w_mem_b = w_mem_kv[0].astype(BF16)
    w_route = jnp.concatenate(
        [w_route_group[0], w_route_expert[0],
         jnp.zeros((d, ROUTE_LANES - N_GROUPS - N_EXPERTS), F32)], axis=1)
    wr_hi, wr_lo = _split_bf16(w_route)

    groups = []
    m_total = x_prompt.shape[0] * x_prompt.shape[1] + x_sample.shape[0] * x_sample.shape[1]
    counts = jnp.zeros((1, ROUTE_LANES), F32)
    h2_all = jnp.zeros((m_total, d), F32)
    row0 = 0
    for x, mem in ((x_prompt, mem_prompt), (x_sample, mem_sample)):
        b, s, _ = x.shape
        tm = _tile(s, 256)
        x2d = x.reshape(b * s, d)
        cos, sin = _rope_tables(s)
        mk, mv = _mem_kv(mem, g_mem[0], w_mem_b, g_mk[0])
        q, k, v, u, cb, mq = _in_proj(x2d, s, g_mix[0], w_in_b, g_q[0], g_k[0], g_mq[0], cos, sin, tm)
        y_attn = _attention(q.reshape(b, s, ATTN_WIDTH), k.reshape(b, s, KV_WIDTH),
                            v.reshape(b, s, KV_WIDTH), _tile(s, 128), _tile(s, 512))
        x1, h2_all, info, counts = _mix_out(
            y_attn.reshape(b * s, ATTN_WIDTH), u, cb, mq, mk, mv, x2d, s, w_o_b, g_out[0], conv_w[0],
            conv_b[0], g_ffn[0], wr_hi, wr_lo, tm, counts, row0, h2_all)
        groups.append((x.shape, x1, info, tm))
        row0 += b * s

    route = jnp.concatenate([g[2][:, :6] for g in groups], axis=0)
    experts_flat = route[:, 0:TOP_K].astype(jnp.int32).reshape(-1)
    rank_flat = route[:, 4:4 + TOP_K].astype(jnp.int32).reshape(-1)
    pos, sorted_tokens, block_expert, block_first, n_used = _dispatch_plan(
        experts_flat, rank_flat, counts[0, :N_EXPERTS].astype(jnp.int32))
    ys = _experts(block_expert, block_first, n_used, sorted_tokens, h2_all, w_gate[0], w_up[0], w_down[0])

    outs = []
    slot0 = 0
    for shape, x1, info, tm in groups:
        m = x1.shape[0]
        out = _combine(pos[slot0:slot0 + m * TOP_K], x1, info, ys, tm)
        outs.append(out.reshape(shape))
        slot0 += m * TOP_K
    return tuple(outs)
```

```python
import functools

import jax
import jax.numpy as jnp
from jax import lax
from jax.experimental import pallas as pl
from jax.experimental.pallas import tpu as pltpu

D_MODEL = 2048
HEAD_DIM = 128
N_Q_HEADS = 8
N_KV_HEADS = 2
Q_PER_KV = N_Q_HEADS // N_KV_HEADS
ATTN_WIDTH = N_Q_HEADS * HEAD_DIM
KV_WIDTH = N_KV_HEADS * HEAD_DIM
CONV_WIDTH = 512
N_MEM_HEADS = 4
MEM_WIDTH = N_MEM_HEADS * HEAD_DIM
MIX_WIDTH = ATTN_WIDTH + CONV_WIDTH + MEM_WIDTH
GRID_W = 64
ROPE_AXIS_DIM = HEAD_DIM // 2
ROPE_THETA = 10000.0
N_GROUPS = 8
EXPERTS_PER_GROUP = 8
N_EXPERTS = N_GROUPS * EXPERTS_PER_GROUP
TOP_K = 2
D_FF_EXPERT = 512
DISPATCH_BLOCK = 256
EPS = 1e-6

ROUTE_LANES = 128
V7X_VMEM_LIMIT = 56 * 1024 * 1024

F32 = jnp.float32
BF16 = jnp.bfloat16
NEG_BIG = -1e30
LOG2_E = 1.4426950408889634
DMA_ISSUE_UNROLL = 8
MIX_SUB_ROWS = 256


def _rms(x, g):
    return x * lax.rsqrt(jnp.mean(x * x, axis=-1, keepdims=True) + EPS) * g


def _cparams(sem, **kw):
    return pltpu.CompilerParams(dimension_semantics=sem, vmem_limit_bytes=V7X_VMEM_LIMIT, **kw)


def _mem_kv_kernel(mem_ref, g_ref, w_ref, gk_ref, mk_ref, mv_ref):
    m = _rms(mem_ref[0], g_ref[...]).astype(BF16)
    kv = jnp.dot(m, w_ref[...], preferred_element_type=F32)
    for h in range(N_MEM_HEADS):
        sl = slice(h * HEAD_DIM, (h + 1) * HEAD_DIM)
        mk_ref[0, :, sl] = _rms(kv[:, sl], gk_ref[...]).astype(BF16)
    mv_ref[0] = kv[:, MEM_WIDTH:].astype(BF16)


def _mem_kv(mem, g_mem, w_mem_kv, g_mk):
    b, n_mem, d = mem.shape
    out = jax.ShapeDtypeStruct((b, n_mem, MEM_WIDTH), BF16)
    return pl.pallas_call(
        _mem_kv_kernel,
        out_shape=(out, out),
        grid=(b,),
        in_specs=[
            pl.BlockSpec((1, n_mem, d), lambda i: (i, 0, 0)),
            pl.BlockSpec((1, d), lambda i: (0, 0)),
            pl.BlockSpec((d, 2 * MEM_WIDTH), lambda i: (0, 0)),
            pl.BlockSpec((1, HEAD_DIM), lambda i: (0, 0)),
        ],
        out_specs=(pl.BlockSpec((1, n_mem, MEM_WIDTH), lambda i: (i, 0, 0)),
                   pl.BlockSpec((1, n_mem, MEM_WIDTH), lambda i: (i, 0, 0))),
        compiler_params=_cparams(("arbitrary",)),
        name="mem_kv",
    )(mem, g_mem.reshape(1, d), w_mem_kv, g_mk.reshape(1, HEAD_DIM))


def _rope_tables(seq_len):
    t = jnp.arange(seq_len, dtype=jnp.int32)
    row = (t // GRID_W).astype(F32)
    col = (t % GRID_W).astype(F32)
    inv_freq = ROPE_THETA ** (-jnp.arange(0, ROPE_AXIS_DIM, 2, dtype=F32) / ROPE_AXIS_DIM)
    ang_r = row[:, None] * inv_freq
    ang_c = col[:, None] * inv_freq
    cos = jnp.concatenate([jnp.cos(ang_r), jnp.cos(ang_r), jnp.cos(ang_c), jnp.cos(ang_c)], axis=-1)
    sin = jnp.concatenate([-jnp.sin(ang_r), jnp.sin(ang_r), -jnp.sin(ang_c), jnp.sin(ang_c)], axis=-1)
    return cos, sin


def _in_proj_kernel(x_ref, gmix_ref, w_ref, gq_ref, gk_ref, gmq_ref, cos_ref, sin_ref,
                    q_ref, k_ref, v_ref, u_ref, cb_ref, mq_ref):
    tm = x_ref.shape[0]
    hb = _rms(x_ref[...], gmix_ref[...]).astype(BF16)
    cos = cos_ref[...]
    sin = sin_ref[...]
    lane = lax.broadcasted_iota(jnp.int32, (tm, HEAD_DIM), 1)
    first_half = (lane % (ROPE_AXIS_DIM)) < (ROPE_AXIS_DIM // 2)
    scale = HEAD_DIM ** -0.5
    q_scale = scale * LOG2_E

    def rope(p):
        swapped = jnp.where(first_half, pltpu.roll(p, HEAD_DIM - ROPE_AXIS_DIM // 2, 1),
                            pltpu.roll(p, ROPE_AXIS_DIM // 2, 1))
        return p * cos + swapped * sin

    def proj(lo, width):
        return jnp.dot(hb, w_ref[:, lo:lo + width], preferred_element_type=F32)

    pq = proj(0, ATTN_WIDTH)
    for h in range(N_Q_HEADS):
        sl = slice(h * HEAD_DIM, (h + 1) * HEAD_DIM)
        q_ref[:, sl] = (rope(_rms(pq[:, sl], gq_ref[...])) * q_scale).astype(BF16)
    pkv = proj(ATTN_WIDTH, 2 * KV_WIDTH)
    for h in range(N_KV_HEADS):
        sl = slice(h * HEAD_DIM, (h + 1) * HEAD_DIM)
        k_ref[:, sl] = rope(_rms(pkv[:, sl], gk_ref[...])).astype(BF16)
    v_ref[...] = pkv[:, KV_WIDTH:].astype(BF16)
    pc = proj(ATTN_WIDTH + 2 * KV_WIDTH, 3 * CONV_WIDTH)
    u_ref[...] = (pc[:, 2 * CONV_WIDTH:] * pc[:, :CONV_WIDTH]).astype(BF16)
    cb_ref[...] = pc[:, CONV_WIDTH:2 * CONV_WIDTH].astype(BF16)
    pm = proj(ATTN_WIDTH + 2 * KV_WIDTH + 3 * CONV_WIDTH, MEM_WIDTH)
    for h in range(N_MEM_HEADS):
        sl = slice(h * HEAD_DIM, (h + 1) * HEAD_DIM)
        mq_ref[:, sl] = (_rms(pm[:, sl], gmq_ref[...]) * scale).astype(BF16)


def _in_proj(x2d, seq_len, g_mix, w_in_b, g_q, g_k, g_mq, cos, sin, tm):
    m, d = x2d.shape
    in_width = w_in_b.shape[1]
    tiles_per_seq = seq_len // tm
    row = lambda i: (i, 0)
    const = lambda i: (0, 0)
    pos = lambda i: (i % tiles_per_seq, 0)
    outs = [(ATTN_WIDTH,), (KV_WIDTH,), (KV_WIDTH,), (CONV_WIDTH,), (CONV_WIDTH,), (MEM_WIDTH,)]
    return pl.pallas_call(
        _in_proj_kernel,
        out_shape=tuple(jax.ShapeDtypeStruct((m, w[0]), BF16) for w in outs),
        grid=(m // tm,),
        in_specs=[
            pl.BlockSpec((tm, d), row),
            pl.BlockSpec((1, d), const),
            pl.BlockSpec((d, in_width), const, pipeline_mode=pl.Buffered(1)),
            pl.BlockSpec((1, HEAD_DIM), const),
            pl.BlockSpec((1, HEAD_DIM), const),
            pl.BlockSpec((1, HEAD_DIM), const),
            pl.BlockSpec((tm, HEAD_DIM), pos),
            pl.BlockSpec((tm, HEAD_DIM), pos),
        ],
        out_specs=tuple(pl.BlockSpec((tm, w[0]), row) for w in outs),
        compiler_params=_cparams(("parallel",)),
        name="in_proj",
    )(x2d, g_mix.reshape(1, d), w_in_b, g_q.reshape(1, HEAD_DIM), g_k.reshape(1, HEAD_DIM),
      g_mq.reshape(1, HEAD_DIM), cos, sin)


def _attn_kernel(q_ref, k_ref, v_ref, o_ref, s_a, s_b, acc_ref, *, tk):
    tq = q_ref.shape[1]
    seq = k_ref.shape[1]
    n = seq // tk
    rows = Q_PER_KV * tq
    q = jnp.concatenate([q_ref[0, :, g * HEAD_DIM:(g + 1) * HEAD_DIM] for g in range(Q_PER_KV)], axis=0)
    ones = jnp.ones((tk, HEAD_DIM), BF16)

    def scores(j, s_ref):
        s = lax.dot_general(q, k_ref[0, j * tk:(j + 1) * tk, :], (((1,), (1,)), ((), ())),
                            preferred_element_type=F32)
        s_ref[...] = s
        return jnp.max(s, axis=1, keepdims=True)

    def update(j, s_ref, mx, m, l):
        m_new = jnp.maximum(m, mx)
        alpha = jnp.exp2(m - m_new)
        p = jnp.exp2(s_ref[...] - m_new).astype(BF16)
        v1 = jnp.concatenate([v_ref[0, j * tk:(j + 1) * tk, :], ones], axis=1)
        pv = jnp.dot(p, v1, preferred_element_type=F32)
        acc_ref[...] = alpha * acc_ref[...] + pv[:, :HEAD_DIM]
        return m_new, alpha * l + pv[:, HEAD_DIM:]

    acc_ref[...] = jnp.zeros_like(acc_ref)
    m0 = jnp.full((rows, 1), NEG_BIG, F32)
    l0 = jnp.zeros((rows, HEAD_DIM), F32)
    bufs = (s_a, s_b)
    m, l = m0, l0
    mx = scores(0, bufs[0])
    for j in range(n):
        mx_next = scores(j + 1, bufs[(j + 1) % 2]) if j + 1 < n else None
        m, l = update(j, bufs[j % 2], mx, m, l)
        mx = mx_next
    o = acc_ref[...] / l
    for g in range(Q_PER_KV):
        o_ref[0, :, g * HEAD_DIM:(g + 1) * HEAD_DIM] = o[g * tq:(g + 1) * tq].astype(BF16)


def _attention(q, k, v, tq, tk):
    b, s, _ = q.shape
    grp_w = Q_PER_KV * HEAD_DIM
    rows = Q_PER_KV * tq
    return pl.pallas_call(
        functools.partial(_attn_kernel, tk=tk),
        out_shape=jax.ShapeDtypeStruct((b, s, ATTN_WIDTH), BF16),
        grid=(b, N_KV_HEADS, s // tq),
        in_specs=[
            pl.BlockSpec((1, tq, grp_w), lambda bi, h, qi: (bi, qi, h)),
            pl.BlockSpec((1, s, HEAD_DIM), lambda bi, h, qi: (bi, 0, h)),
            pl.BlockSpec((1, s, HEAD_DIM), lambda bi, h, qi: (bi, 0, h)),
        ],
        out_specs=pl.BlockSpec((1, tq, grp_w), lambda bi, h, qi: (bi, qi, h)),
        scratch_shapes=[pltpu.VMEM((rows, tk), F32), pltpu.VMEM((rows, tk), F32),
                        pltpu.VMEM((rows, HEAD_DIM), F32)],
        compiler_params=_cparams(("parallel", "parallel", "parallel")),
        name="attention",
    )(q, k, v)


def _route(logits):
    tm = logits.shape[0]
    lane = lax.broadcasted_iota(jnp.int32, (tm, ROUTE_LANES), 1)
    lane_f = lane.astype(F32)
    no_lane = float(ROUTE_LANES)
    is_g = lane < N_GROUPS
    gl = jnp.where(is_g, logits, NEG_BIG)
    gmax = jnp.max(gl, axis=1, keepdims=True)
    gsum = jnp.sum(jnp.where(is_g, jnp.exp(gl - gmax), 0.0), axis=1, keepdims=True)
    p_group = 1.0 / gsum
    gidx = jnp.min(jnp.where(is_g & (gl == gmax), lane_f, no_lane), axis=1, keepdims=True)
    lo = N_GROUPS + EXPERTS_PER_GROUP * gidx
    in_grp = (lane_f >= lo) & (lane_f < lo + EXPERTS_PER_GROUP)
    el = jnp.where(in_grp, logits, NEG_BIG)
    emax = jnp.max(el, axis=1, keepdims=True)
    j1 = jnp.min(jnp.where(in_grp & (el == emax), lane_f, no_lane), axis=1, keepdims=True)
    rest = in_grp & (lane_f != j1)
    el2 = jnp.where(rest, logits, NEG_BIG)
    e2max = jnp.max(el2, axis=1, keepdims=True)
    j2 = jnp.min(jnp.where(rest & (el2 == e2max), lane_f, no_lane), axis=1, keepdims=True)
    r = jnp.exp(e2max - emax)
    g1 = p_group / (1.0 + r)
    g2 = g1 * r
    return j1 - N_GROUPS, j2 - N_GROUPS, g1, g2


def _slot_ranks(e1, e2, counts):
    tm = e1.shape[0]
    lane_f = lax.broadcasted_iota(jnp.int32, (tm, ROUTE_LANES), 1).astype(F32)
    oh1 = lane_f == e1
    oh2 = lane_f == e2
    both = jnp.where(oh1 | oh2, 1.0, 0.0)
    earlier = (lax.broadcasted_iota(jnp.int32, (tm, tm), 1)
               < lax.broadcasted_iota(jnp.int32, (tm, tm), 0))
    before = jnp.dot(jnp.where(earlier, 1.0, 0.0).astype(BF16), both.astype(BF16),
                     preferred_element_type=F32) + counts
    rank1 = jnp.sum(jnp.where(oh1, before, 0.0), axis=1, keepdims=True)
    rank2 = jnp.sum(jnp.where(oh2, before, 0.0), axis=1, keepdims=True)
    return rank1, rank2, counts + jnp.sum(both, axis=0, keepdims=True)


def _split_bf16(a):
    hi = a.astype(BF16)
    lo = (a - hi.astype(F32)).astype(BF16)
    return hi, lo


def _mix_out_kernel(ya_ref, u_ref, up_ref, un_ref, cb_ref, mq_ref, mk_ref, mv_ref, x_ref,
                    wo_ref, gout_ref, convw_ref, convb_ref, gffn_ref, wr_hi_ref, wr_lo_ref, cnt_in_ref,
                    h2_buf_ref, x1_ref, h2_ref, info_ref, cnt_out_ref, cnt_ref, *, tiles_per_seq):
    del h2_buf_ref
    tm = x_ref.shape[0]
    i = pl.program_id(0)
    t_in_seq = i % tiles_per_seq

    @pl.when(i == 0)
    def _():
        cnt_ref[...] = cnt_in_ref[...]

    u = u_ref[...].astype(F32)
    rowi = lax.broadcasted_iota(jnp.int32, (tm, CONV_WIDTH), 0)
    prev_row = jnp.where(t_in_seq == 0, 0.0, up_ref[7:8, :].astype(F32))
    next_row = jnp.where(t_in_seq == tiles_per_seq - 1, 0.0, un_ref[0:1, :].astype(F32))
    u_prev = jnp.where(rowi == 0, prev_row, pltpu.roll(u, 1, 0))
    u_next = jnp.where(rowi == tm - 1, next_row, pltpu.roll(u, tm - 1, 0))
    conv = u_prev * convw_ref[0:1, :] + u * convw_ref[1:2, :] + u_next * convw_ref[2:3, :] + convb_ref[...]
    c_all = _rms(cb_ref[...].astype(F32) * conv, gout_ref[:, ATTN_WIDTH:ATTN_WIDTH + CONV_WIDTH])

    sub_rows = min(MIX_SUB_ROWS, tm)
    counts = cnt_ref[...]
    for r0 in range(0, tm, sub_rows):
        rs = slice(r0, r0 + sub_rows)
        a = _rms(ya_ref[rs, :].astype(F32), gout_ref[:, :ATTN_WIDTH])
        heads = []
        for h in range(N_MEM_HEADS):
            sl = slice(h * HEAD_DIM, (h + 1) * HEAD_DIM)
            s = lax.dot_general(mq_ref[rs, sl], mk_ref[0, :, sl], (((1,), (1,)), ((), ())),
                                preferred_element_type=F32)
            p = jnp.exp(s - jnp.max(s, axis=1, keepdims=True))
            o = jnp.dot(p.astype(BF16), mv_ref[0, :, sl], preferred_element_type=F32)
            heads.append(o / jnp.sum(p, axis=1, keepdims=True))
        ym = _rms(jnp.concatenate(heads, axis=1), gout_ref[:, ATTN_WIDTH + CONV_WIDTH:])

        mixed = jnp.concatenate([a, c_all[rs], ym], axis=1).astype(BF16)
        x1 = x_ref[rs, :] + jnp.dot(mixed, wo_ref[...], preferred_element_type=F32)
        x1_ref[rs, :] = x1
        h2 = _rms(x1, gffn_ref[...])
        h2_ref[rs, :] = h2
        h_hi, h_lo = _split_bf16(h2)
        logits = (jnp.dot(h_hi, wr_hi_ref[...], preferred_element_type=F32)
                  + jnp.dot(h_hi, wr_lo_ref[...], preferred_element_type=F32)
                  + jnp.dot(h_lo, wr_hi_ref[...], preferred_element_type=F32))
        e1, e2, g1, g2 = _route(logits)
        rank1, rank2, counts = _slot_ranks(e1, e2, counts)
        lane = lax.broadcasted_iota(jnp.int32, (sub_rows, ROUTE_LANES), 1)
        info = jnp.zeros((sub_rows, ROUTE_LANES), F32)
        for k, col in enumerate((e1, e2, g1, g2, rank1, rank2)):
            info = jnp.where(lane == k, col, info)
        info_ref[rs, :] = info
    cnt_ref[...] = counts
    cnt_out_ref[...] = counts


def _mix_out(y_attn, u, cb, mq, mk, mv, x2d, seq_len, w_o_b, g_out, conv_w, conv_b, g_ffn,
             wr_hi, wr_lo, tm, counts_in, h2_row0, h2_buf):
    m, d = x2d.shape
    n_mem = mk.shape[1]
    tiles_per_seq = seq_len // tm
    sub = tm // 8
    n_sub = m // 8
    h2_tile0 = h2_row0 // tm
    row = lambda i: (i, 0)
    const = lambda i: (0, 0)
    batch = lambda i: (i // tiles_per_seq, 0, 0)
    h2_buf_arg = 17
    extra_specs, extra_args = [pl.BlockSpec(memory_space=pl.ANY)], [h2_buf]
    return pl.pallas_call(
        functools.partial(_mix_out_kernel, tiles_per_seq=tiles_per_seq),
        out_shape=(jax.ShapeDtypeStruct((m, d), F32), jax.ShapeDtypeStruct(h2_buf.shape, F32),
                   jax.ShapeDtypeStruct((m, ROUTE_LANES), F32),
                   jax.ShapeDtypeStruct((1, ROUTE_LANES), F32)),
        grid=(m // tm,),
        input_output_aliases={h2_buf_arg: 1},
        scratch_shapes=[pltpu.VMEM((1, ROUTE_LANES), F32)],
        in_specs=[
            pl.BlockSpec((tm, ATTN_WIDTH), row),
            pl.BlockSpec((tm, CONV_WIDTH), row),
            pl.BlockSpec((8, CONV_WIDTH), lambda i: (jnp.maximum(i * sub - 1, 0), 0)),
            pl.BlockSpec((8, CONV_WIDTH), lambda i: (jnp.minimum((i + 1) * sub, n_sub - 1), 0)),
            pl.BlockSpec((tm, CONV_WIDTH), row),
            pl.BlockSpec((tm, MEM_WIDTH), row),
            pl.BlockSpec((1, n_mem, MEM_WIDTH), batch),
            pl.BlockSpec((1, n_mem, MEM_WIDTH), batch),
            pl.BlockSpec((tm, d), row),
            pl.BlockSpec((MIX_WIDTH, d), const, pipeline_mode=pl.Buffered(1)),
            pl.BlockSpec((1, MIX_WIDTH), const),
            pl.BlockSpec((3, CONV_WIDTH), const),
            pl.BlockSpec((1, CONV_WIDTH), const),
            pl.BlockSpec((1, d), const),
            pl.BlockSpec((d, ROUTE_LANES), const),
            pl.BlockSpec((d, ROUTE_LANES), const),
            pl.BlockSpec((1, ROUTE_LANES), const),
        ] + extra_specs,
        out_specs=(pl.BlockSpec((tm, d), row), pl.BlockSpec((tm, d), lambda i: (i + h2_tile0, 0)),
                   pl.BlockSpec((tm, ROUTE_LANES), row), pl.BlockSpec((1, ROUTE_LANES), const)),
        compiler_params=_cparams(("arbitrary",)),
        name="mix_out",
    )(y_attn, u, u, u, cb, mq, mk, mv, x2d, w_o_b, g_out.reshape(1, MIX_WIDTH), conv_w,
      conv_b.reshape(1, CONV_WIDTH), g_ffn.reshape(1, d), wr_hi, wr_lo, counts_in, *extra_args)


def _row_copy(src_hbm, idx, dst, row, sem):
    return pltpu.make_async_copy(src_hbm.at[pl.ds(idx, 1), :], dst.at[pl.ds(row, 1), :], sem)


def _experts_kernel(bexp_ref, bfirst_ref, bwslot_ref, bnext_ref, nused_ref, tok_ref,
                    h2_hbm, wg_hbm, wu_hbm, wd_hbm, ys_ref,
                    xbuf, sem, x_b, wg_f, wu_f, wd_f, wsem, wg_b, wu_b, wd_b):
    i = pl.program_id(0)
    n_used = nused_ref[0]
    blk = DISPATCH_BLOCK

    def start_gather(b, slot):
        first = bfirst_ref[b]

        def body(r, carry):
            _row_copy(h2_hbm, tok_ref[first + r], xbuf.at[slot], r, sem.at[slot]).start()
            return carry
        lax.fori_loop(0, blk, body, 0, unroll=DMA_ISSUE_UNROLL)

    def wait_gather(slot):
        pltpu.make_async_copy(h2_hbm.at[pl.ds(0, blk), :], xbuf.at[slot], sem.at[slot]).wait()

    def weight_copies(e, wslot):
        return [pltpu.make_async_copy(src.at[e], dst.at[wslot], wsem.at[wslot])
                for src, dst in ((wg_hbm, wg_f), (wu_hbm, wu_f), (wd_hbm, wd_f))]

    @pl.when(i == 0)
    def _():
        start_gather(0, 0)
        for c in weight_copies(bexp_ref[0], 0):
            c.start()

    @pl.when(i < n_used)
    def _():
        slot = i % 2
        wait_gather(slot)

        @pl.when((i == 0) | (bexp_ref[i] != bexp_ref[jnp.maximum(i - 1, 0)]))
        def _():
            wslot = bwslot_ref[i]
            for c in weight_copies(bexp_ref[i], wslot):
                c.wait()
            wg_b[...] = wg_f[wslot].astype(BF16)
            wu_b[...] = wu_f[wslot].astype(BF16)
            wd_b[...] = wd_f[wslot].astype(BF16)

            @pl.when(bnext_ref[i] >= 0)
            def _():
                for c in weight_copies(bnext_ref[i], 1 - wslot):
                    c.start()

        x_b[...] = xbuf[slot].astype(BF16)
        nxt = bfirst_ref[jnp.minimum(i + 1, n_used - 1)]
        for r in range(blk):
            _row_copy(h2_hbm, tok_ref[nxt + r], xbuf.at[1 - slot], r, sem.at[1 - slot]).start()

        x = x_b[...]
        g = jnp.dot(x, wg_b[...], preferred_element_type=F32)
        up = jnp.dot(x, wu_b[...], preferred_element_type=F32)
        hid = (g / (1.0 + jnp.exp(-g))) * up
        ys_ref[...] = jnp.dot(hid.astype(BF16), wd_b[...], preferred_element_type=F32)

        @pl.when(i == n_used - 1)
        def _():
            wait_gather(1 - slot)

    @pl.when(i >= n_used)
    def _():
        ys_ref[...] = jnp.zeros_like(ys_ref)


def _experts(plan, sorted_tokens, h2, w_gate, w_up, w_down):
    block_expert, block_first, block_wslot, block_next, n_used = plan
    n_blocks = block_expert.shape[0]
    d = h2.shape[1]
    blk = DISPATCH_BLOCK
    hbm = pl.BlockSpec(memory_space=pl.ANY)
    grid_spec = pltpu.PrefetchScalarGridSpec(
        num_scalar_prefetch=6,
        grid=(n_blocks,),
        in_specs=[hbm, hbm, hbm, hbm],
        out_specs=pl.BlockSpec((blk, d), lambda i, *_: (i, 0)),
        scratch_shapes=[
            pltpu.VMEM((2, blk, d), F32),
            pltpu.SemaphoreType.DMA((2,)),
            pltpu.VMEM((blk, d), BF16),
            pltpu.VMEM((2, d, D_FF_EXPERT), F32),
            pltpu.VMEM((2, d, D_FF_EXPERT), F32),
            pltpu.VMEM((2, D_FF_EXPERT, d), F32),
            pltpu.SemaphoreType.DMA((2,)),
            pltpu.VMEM((d, D_FF_EXPERT), BF16),
            pltpu.VMEM((d, D_FF_EXPERT), BF16),
            pltpu.VMEM((D_FF_EXPERT, d), BF16),
        ],
    )
    return pl.pallas_call(
        _experts_kernel,
        out_shape=jax.ShapeDtypeStruct((n_blocks * blk, d), F32),
        grid_spec=grid_spec,
        compiler_params=_cparams(("arbitrary",)),
        name="experts",
    )(block_expert, block_first, block_wslot, block_next, n_used, sorted_tokens, h2, w_gate, w_up, w_down)


def _combine_kernel(pos_ref, x1_ref, info_ref, ys_hbm, out_ref, ybuf, sem):
    i = pl.program_id(0)
    n = pl.num_programs(0)
    tm = x1_ref.shape[0]

    def start_gather(t, slot):
        def body(r, carry):
            base = (t * tm + r) * TOP_K
            for k in range(TOP_K):
                _row_copy(ys_hbm, pos_ref[base + k], ybuf.at[slot, k], r, sem.at[slot]).start()
            return carry
        lax.fori_loop(0, tm, body, 0, unroll=DMA_ISSUE_UNROLL // TOP_K)

    @pl.when(i == 0)
    def _():
        start_gather(0, 0)

    @pl.when(i + 1 < n)
    def _():
        start_gather(i + 1, (i + 1) % 2)

    slot = i % 2
    for k in range(TOP_K):
        pltpu.make_async_copy(ys_hbm.at[pl.ds(0, tm), :], ybuf.at[slot, k], sem.at[slot]).wait()
    info = info_ref[...]
    out_ref[...] = (x1_ref[...] + ybuf[slot, 0] * info[:, 2:3] + ybuf[slot, 1] * info[:, 3:4])


def _combine(pos, x1, info, ys, tm):
    m, d = x1.shape
    grid_spec = pltpu.PrefetchScalarGridSpec(
        num_scalar_prefetch=1,
        grid=(m // tm,),
        in_specs=[
            pl.BlockSpec((tm, d), lambda i, pos: (i, 0)),
            pl.BlockSpec((tm, ROUTE_LANES), lambda i, pos: (i, 0)),
            pl.BlockSpec(memory_space=pl.ANY),
        ],
        out_specs=pl.BlockSpec((tm, d), lambda i, pos: (i, 0)),
        scratch_shapes=[pltpu.VMEM((2, TOP_K, tm, d), F32), pltpu.SemaphoreType.DMA((2,))],
    )
    return pl.pallas_call(
        _combine_kernel,
        out_shape=jax.ShapeDtypeStruct((m, d), F32),
        grid_spec=grid_spec,
        compiler_params=_cparams(("arbitrary",)),
        name="combine",
    )(pos, x1, info, ys)


def _dispatch_plan(experts_flat, rank_flat, counts):
    n_slots = experts_flat.shape[0]
    blk = DISPATCH_BLOCK
    n_blocks = -(-n_slots // blk) + N_EXPERTS
    rank = rank_flat
    padded = ((counts + blk - 1) // blk) * blk
    padded_end = jnp.cumsum(padded)
    padded_start = padded_end - padded
    e2d = experts_flat.reshape(-1, ROUTE_LANES)
    start_of_slot = jnp.zeros_like(e2d)
    for e in range(N_EXPERTS):
        start_of_slot = jnp.where(e2d == e, padded_start[e], start_of_slot)
    pos = (start_of_slot.reshape(-1) + rank).astype(jnp.int32)
    sorted_tokens = jnp.concatenate([jnp.argsort(pos).astype(jnp.int32) // TOP_K,
                                     jnp.zeros((blk,), jnp.int32)])
    n_used = (padded_end[-1] // blk).astype(jnp.int32)
    block_start = jnp.minimum(jnp.arange(n_blocks, dtype=jnp.int32) * blk,
                              jnp.maximum(padded_end[-1] - blk, 0))
    block_expert = jnp.sum(padded_end[None, :] <= block_start[:, None], axis=1)
    block_expert = jnp.minimum(block_expert, N_EXPERTS - 1).astype(jnp.int32)
    start = jnp.cumsum(counts) - counts
    block_first = (block_start - (padded_start - start)[block_expert]).astype(jnp.int32)
    changed = jnp.concatenate([jnp.zeros((1,), jnp.int32),
                               (block_expert[1:] != block_expert[:-1]).astype(jnp.int32)])
    block_wslot = (jnp.cumsum(changed) % 2).astype(jnp.int32)
    none = N_EXPERTS
    ids = jnp.where(counts > 0, jnp.arange(N_EXPERTS, dtype=jnp.int32), none)
    next_incl = lax.cummin(ids, axis=0, reverse=True)
    next_excl = jnp.concatenate([next_incl[1:], jnp.full((1,), none, jnp.int32)])
    block_next = jnp.where(next_excl < none, next_excl, -1)[block_expert].astype(jnp.int32)
    return pos, sorted_tokens, (block_expert, block_first, block_wslot, block_next, n_used.reshape(1))


def _tile(n, pref):
    return pref if n % pref == 0 else n


def kernel(x_prompt, x_sample, mem_prompt, mem_sample, g_mix, w_in, g_q, g_k, conv_w, conv_b, g_mem, w_mem_kv, g_mq, g_mk, g_out, w_o, g_ffn, w_route_group, w_route_expert, w_gate, w_up, w_down):
    assert g_mix.shape[0] == 1, "single layer"
    d = D_MODEL
    w_in_b = w_in[0].astype(BF16)
    w_o_b = w_o[0].astype(BF16)
    w_mem_b = w_mem_kv[0].astype(BF16)
    w_route = jnp.concatenate(
        [w_route_group[0], w_route_expert[0],
         jnp.zeros((d, ROUTE_LANES - N_GROUPS - N_EXPERTS), F32)], axis=1)
    wr_hi, wr_lo = _split_bf16(w_route)

    groups = []
    m_total = x_prompt.shape[0] * x_prompt.shape[1] + x_sample.shape[0] * x_sample.shape[1]
    counts = jnp.zeros((1, ROUTE_LANES), F32)
    h2_all = jnp.zeros((m_total, d), F32)
    row0 = 0
    for x, mem in ((x_prompt, mem_prompt), (x_sample, mem_sample)):
        b, s, _ = x.shape
        tm = _tile(s, 256)
        x2d = x.reshape(b * s, d)
        cos, sin = _rope_tables(s)
        mk, mv = _mem_kv(mem, g_mem[0], w_mem_b, g_mk[0])
        q, k, v, u, cb, mq = _in_proj(x2d, s, g_mix[0], w_in_b, g_q[0], g_k[0], g_mq[0], cos, sin, tm)
        y_attn = _attention(q.reshape(b, s, ATTN_WIDTH), k.reshape(b, s, KV_WIDTH),
                            v.reshape(b, s, KV_WIDTH), _tile(s, 256), _tile(s, 512))
        x1, h2_all, info, counts = _mix_out(
            y_attn.reshape(b * s, ATTN_WIDTH), u, cb, mq, mk, mv, x2d, s, w_o_b, g_out[0], conv_w[0],
            conv_b[0], g_ffn[0], wr_hi, wr_lo, _tile(s, 2 * MIX_SUB_ROWS), counts, row0, h2_all)
        groups.append((x.shape, x1, info, tm))
        row0 += b * s

    route = jnp.concatenate([g[2][:, :6] for g in groups], axis=0)
    experts_flat = route[:, 0:TOP_K].astype(jnp.int32).reshape(-1)
    rank_flat = route[:, 4:4 + TOP_K].astype(jnp.int32).reshape(-1)
    pos, sorted_tokens, plan = _dispatch_plan(
        experts_flat, rank_flat, counts[0, :N_EXPERTS].astype(jnp.int32))
    ys = _experts(plan, sorted_tokens, h2_all, w_gate[0], w_up[0], w_down[0])

    outs = []
    slot0 = 0
    for shape, x1, info, tm in groups:
        m = x1.shape[0]
        out = _combine(pos[slot0:slot0 + m * TOP_K], x1, info, ys, tm)
        outs.append(out.reshape(shape))
        slot0 += m * TOP_K
    return tuple(outs)
```

```python
import functools

import jax
import jax.numpy as jnp
from jax import lax
from jax.experimental import pallas as pl
from jax.experimental.pallas import tpu as pltpu

D_MODEL = 2048
HEAD_DIM = 128
N_Q_HEADS = 8
N_KV_HEADS = 2
Q_PER_KV = N_Q_HEADS // N_KV_HEADS
ATTN_WIDTH = N_Q_HEADS * HEAD_DIM
KV_WIDTH = N_KV_HEADS * HEAD_DIM
CONV_WIDTH = 512
N_MEM_HEADS = 4
MEM_WIDTH = N_MEM_HEADS * HEAD_DIM
MIX_WIDTH = ATTN_WIDTH + CONV_WIDTH + MEM_WIDTH
GRID_W = 64
ROPE_AXIS_DIM = HEAD_DIM // 2
ROPE_THETA = 10000.0
N_GROUPS = 8
EXPERTS_PER_GROUP = 8
N_EXPERTS = N_GROUPS * EXPERTS_PER_GROUP
TOP_K = 2
D_FF_EXPERT = 512
DISPATCH_BLOCK = 256
EPS = 1e-6

LANES = 128
ROUTE_LANES = LANES
V7X_VMEM_LIMIT = 56 * 1024 * 1024

F32 = jnp.float32
BF16 = jnp.bfloat16
NEG_BIG = -1e30
LOG2_E = 1.4426950408889634
DMA_ISSUE_UNROLL = 8
N_DMA_QUEUES = 2
MIX_SUB_ROWS = 256


def _rms(x, g):
    return x * lax.rsqrt(jnp.mean(x * x, axis=-1, keepdims=True) + EPS) * g


def _cparams(sem, **kw):
    return pltpu.CompilerParams(dimension_semantics=sem, vmem_limit_bytes=V7X_VMEM_LIMIT, **kw)


def _mem_kv_kernel(mem_ref, g_ref, w_ref, gk_ref, mk_ref, mv_ref):
    m = _rms(mem_ref[0], g_ref[...]).astype(BF16)
    kv = jnp.dot(m, w_ref[...], preferred_element_type=F32)
    for h in range(N_MEM_HEADS):
        sl = slice(h * HEAD_DIM, (h + 1) * HEAD_DIM)
        mk_ref[0, :, sl] = _rms(kv[:, sl], gk_ref[...]).astype(BF16)
    mv_ref[0] = kv[:, MEM_WIDTH:].astype(BF16)


def _mem_kv(mem, g_mem, w_mem_kv, g_mk):
    b, n_mem, d = mem.shape
    out = jax.ShapeDtypeStruct((b, n_mem, MEM_WIDTH), BF16)
    return pl.pallas_call(
        _mem_kv_kernel,
        out_shape=(out, out),
        grid=(b,),
        in_specs=[
            pl.BlockSpec((1, n_mem, d), lambda i: (i, 0, 0)),
            pl.BlockSpec((1, d), lambda i: (0, 0)),
            pl.BlockSpec((d, 2 * MEM_WIDTH), lambda i: (0, 0)),
            pl.BlockSpec((1, HEAD_DIM), lambda i: (0, 0)),
        ],
        out_specs=(pl.BlockSpec((1, n_mem, MEM_WIDTH), lambda i: (i, 0, 0)),
                   pl.BlockSpec((1, n_mem, MEM_WIDTH), lambda i: (i, 0, 0))),
        compiler_params=_cparams(("arbitrary",)),
        name="mem_kv",
    )(mem, g_mem.reshape(1, d), w_mem_kv, g_mk.reshape(1, HEAD_DIM))


def _rope_tables(seq_len):
    t = jnp.arange(seq_len, dtype=jnp.int32)
    row = (t // GRID_W).astype(F32)
    col = (t % GRID_W).astype(F32)
    inv_freq = ROPE_THETA ** (-jnp.arange(0, ROPE_AXIS_DIM, 2, dtype=F32) / ROPE_AXIS_DIM)
    ang_r = row[:, None] * inv_freq
    ang_c = col[:, None] * inv_freq
    cos = jnp.concatenate([jnp.cos(ang_r), jnp.cos(ang_r), jnp.cos(ang_c), jnp.cos(ang_c)], axis=-1)
    sin = jnp.concatenate([-jnp.sin(ang_r), jnp.sin(ang_r), -jnp.sin(ang_c), jnp.sin(ang_c)], axis=-1)
    return cos, sin


def _in_proj_kernel(x_ref, gmix_ref, w_ref, gq_ref, gk_ref, gmq_ref, cos_ref, sin_ref,
                    q_ref, k_ref, v_ref, u_ref, cb_ref, mq_ref):
    tm = x_ref.shape[0]
    hb = _rms(x_ref[...], gmix_ref[...]).astype(BF16)
    cos = cos_ref[...]
    sin = sin_ref[...]
    lane = lax.broadcasted_iota(jnp.int32, (tm, HEAD_DIM), 1)
    first_half = (lane % (ROPE_AXIS_DIM)) < (ROPE_AXIS_DIM // 2)
    scale = HEAD_DIM ** -0.5
    q_scale = scale * LOG2_E

    def rope(p):
        swapped = jnp.where(first_half, pltpu.roll(p, HEAD_DIM - ROPE_AXIS_DIM // 2, 1),
                            pltpu.roll(p, ROPE_AXIS_DIM // 2, 1))
        return p * cos + swapped * sin

    def proj(lo, width):
        return jnp.dot(hb, w_ref[:, lo:lo + width], preferred_element_type=F32)

    pq = proj(0, ATTN_WIDTH)
    for h in range(N_Q_HEADS):
        sl = slice(h * HEAD_DIM, (h + 1) * HEAD_DIM)
        q_ref[:, sl] = (rope(_rms(pq[:, sl], gq_ref[...])) * q_scale).astype(BF16)
    pkv = proj(ATTN_WIDTH, 2 * KV_WIDTH)
    for h in range(N_KV_HEADS):
        sl = slice(h * HEAD_DIM, (h + 1) * HEAD_DIM)
        k_ref[:, sl] = rope(_rms(pkv[:, sl], gk_ref[...])).astype(BF16)
    v_ref[...] = pkv[:, KV_WIDTH:].astype(BF16)
    pc = proj(ATTN_WIDTH + 2 * KV_WIDTH, 3 * CONV_WIDTH)
    u_ref[...] = (pc[:, 2 * CONV_WIDTH:] * pc[:, :CONV_WIDTH]).astype(BF16)
    cb_ref[...] = pc[:, CONV_WIDTH:2 * CONV_WIDTH].astype(BF16)
    pm = proj(ATTN_WIDTH + 2 * KV_WIDTH + 3 * CONV_WIDTH, MEM_WIDTH)
    for h in range(N_MEM_HEADS):
        sl = slice(h * HEAD_DIM, (h + 1) * HEAD_DIM)
        mq_ref[:, sl] = (_rms(pm[:, sl], gmq_ref[...]) * scale).astype(BF16)


def _in_proj(x2d, seq_len, g_mix, w_in_b, g_q, g_k, g_mq, cos, sin, tm):
    m, d = x2d.shape
    in_width = w_in_b.shape[1]
    tiles_per_seq = seq_len // tm
    row = lambda i: (i, 0)
    const = lambda i: (0, 0)
    pos = lambda i: (i % tiles_per_seq, 0)
    outs = [(ATTN_WIDTH,), (KV_WIDTH,), (KV_WIDTH,), (CONV_WIDTH,), (CONV_WIDTH,), (MEM_WIDTH,)]
    return pl.pallas_call(
        _in_proj_kernel,
        out_shape=tuple(jax.ShapeDtypeStruct((m, w[0]), BF16) for w in outs),
        grid=(m // tm,),
        in_specs=[
            pl.BlockSpec((tm, d), row),
            pl.BlockSpec((1, d), const),
            pl.BlockSpec((d, in_width), const, pipeline_mode=pl.Buffered(1)),
            pl.BlockSpec((1, HEAD_DIM), const),
            pl.BlockSpec((1, HEAD_DIM), const),
            pl.BlockSpec((1, HEAD_DIM), const),
            pl.BlockSpec((tm, HEAD_DIM), pos),
            pl.BlockSpec((tm, HEAD_DIM), pos),
        ],
        out_specs=tuple(pl.BlockSpec((tm, w[0]), row) for w in outs),
        compiler_params=_cparams(("parallel",)),
        name="in_proj",
    )(x2d, g_mix.reshape(1, d), w_in_b, g_q.reshape(1, HEAD_DIM), g_k.reshape(1, HEAD_DIM),
      g_mq.reshape(1, HEAD_DIM), cos, sin)


def _attn_kernel(q_ref, k_ref, v_ref, o_ref, s_a, s_b, acc_ref, *, tk):
    tq = q_ref.shape[1]
    seq = k_ref.shape[1]
    n = seq // tk
    rows = Q_PER_KV * tq
    q = jnp.concatenate([q_ref[0, :, g * HEAD_DIM:(g + 1) * HEAD_DIM] for g in range(Q_PER_KV)], axis=0)
    ones = jnp.ones((tk, HEAD_DIM), BF16)

    def scores(j, s_ref):
        s = lax.dot_general(q, k_ref[0, j * tk:(j + 1) * tk, :], (((1,), (1,)), ((), ())),
                            preferred_element_type=F32)
        s_ref[...] = s
        return jnp.max(s, axis=1, keepdims=True)

    def update(j, s_ref, mx, m, l):
        m_new = jnp.maximum(m, mx)
        alpha = jnp.exp2(m - m_new)
        p = jnp.exp2(s_ref[...] - m_new).astype(BF16)
        v1 = jnp.concatenate([v_ref[0, j * tk:(j + 1) * tk, :], ones], axis=1)
        pv = jnp.dot(p, v1, preferred_element_type=F32)
        acc_ref[...] = alpha * acc_ref[...] + pv[:, :HEAD_DIM]
        return m_new, alpha * l + pv[:, HEAD_DIM:]

    acc_ref[...] = jnp.zeros_like(acc_ref)
    m0 = jnp.full((rows, 1), NEG_BIG, F32)
    l0 = jnp.zeros((rows, HEAD_DIM), F32)
    bufs = (s_a, s_b)
    m, l = m0, l0
    mx = scores(0, bufs[0])
    for j in range(n):
        mx_next = scores(j + 1, bufs[(j + 1) % 2]) if j + 1 < n else None
        m, l = update(j, bufs[j % 2], mx, m, l)
        mx = mx_next
    o = acc_ref[...] / l
    for g in range(Q_PER_KV):
        o_ref[0, :, g * HEAD_DIM:(g + 1) * HEAD_DIM] = o[g * tq:(g + 1) * tq].astype(BF16)


def _attention(q, k, v, tq, tk):
    b, s, _ = q.shape
    grp_w = Q_PER_KV * HEAD_DIM
    rows = Q_PER_KV * tq
    return pl.pallas_call(
        functools.partial(_attn_kernel, tk=tk),
        out_shape=jax.ShapeDtypeStruct((b, s, ATTN_WIDTH), BF16),
        grid=(b, N_KV_HEADS, s // tq),
        in_specs=[
            pl.BlockSpec((1, tq, grp_w), lambda bi, h, qi: (bi, qi, h)),
            pl.BlockSpec((1, s, HEAD_DIM), lambda bi, h, qi: (bi, 0, h)),
            pl.BlockSpec((1, s, HEAD_DIM), lambda bi, h, qi: (bi, 0, h)),
        ],
        out_specs=pl.BlockSpec((1, tq, grp_w), lambda bi, h, qi: (bi, qi, h)),
        scratch_shapes=[pltpu.VMEM((rows, tk), F32), pltpu.VMEM((rows, tk), F32),
                        pltpu.VMEM((rows, HEAD_DIM), F32)],
        compiler_params=_cparams(("parallel", "parallel", "parallel")),
        name="attention",
    )(q, k, v)


def _route(logits):
    tm = logits.shape[0]
    lane = lax.broadcasted_iota(jnp.int32, (tm, ROUTE_LANES), 1)
    lane_f = lane.astype(F32)
    no_lane = float(ROUTE_LANES)
    is_g = lane < N_GROUPS
    gl = jnp.where(is_g, logits, NEG_BIG)
    gmax = jnp.max(gl, axis=1, keepdims=True)
    gsum = jnp.sum(jnp.where(is_g, jnp.exp(gl - gmax), 0.0), axis=1, keepdims=True)
    p_group = 1.0 / gsum
    gidx = jnp.min(jnp.where(is_g & (gl == gmax), lane_f, no_lane), axis=1, keepdims=True)
    lo = N_GROUPS + EXPERTS_PER_GROUP * gidx
    in_grp = (lane_f >= lo) & (lane_f < lo + EXPERTS_PER_GROUP)
    el = jnp.where(in_grp, logits, NEG_BIG)
    emax = jnp.max(el, axis=1, keepdims=True)
    j1 = jnp.min(jnp.where(in_grp & (el == emax), lane_f, no_lane), axis=1, keepdims=True)
    rest = in_grp & (lane_f != j1)
    el2 = jnp.where(rest, logits, NEG_BIG)
    e2max = jnp.max(el2, axis=1, keepdims=True)
    j2 = jnp.min(jnp.where(rest & (el2 == e2max), lane_f, no_lane), axis=1, keepdims=True)
    r = jnp.exp(e2max - emax)
    g1 = p_group / (1.0 + r)
    g2 = g1 * r
    return j1 - N_GROUPS, j2 - N_GROUPS, g1, g2


def _slot_ranks(e1, e2, counts):
    tm = e1.shape[0]
    lane_f = lax.broadcasted_iota(jnp.int32, (tm, ROUTE_LANES), 1).astype(F32)
    oh1 = lane_f == e1
    oh2 = lane_f == e2
    both = jnp.where(oh1 | oh2, 1.0, 0.0)
    earlier = (lax.broadcasted_iota(jnp.int32, (tm, tm), 1)
               < lax.broadcasted_iota(jnp.int32, (tm, tm), 0))
    before = jnp.dot(jnp.where(earlier, 1.0, 0.0).astype(BF16), both.astype(BF16),
                     preferred_element_type=F32) + counts
    rank1 = jnp.sum(jnp.where(oh1, before, 0.0), axis=1, keepdims=True)
    rank2 = jnp.sum(jnp.where(oh2, before, 0.0), axis=1, keepdims=True)
    return rank1, rank2, counts + jnp.sum(both, axis=0, keepdims=True)


def _split_bf16(a):
    hi = a.astype(BF16)
    lo = (a - hi.astype(F32)).astype(BF16)
    return hi, lo


def _mix_out_kernel(ya_ref, u_ref, up_ref, un_ref, cb_ref, mq_ref, mk_ref, mv_ref, x_ref,
                    wo_ref, gout_ref, convw_ref, convb_ref, gffn_ref, wr_hi_ref, wr_lo_ref, cnt_in_ref,
                    h2_buf_ref, x1_ref, h2_ref, info_ref, cnt_out_ref, cnt_ref, *, tiles_per_seq):
    del h2_buf_ref
    tm = x_ref.shape[0]
    i = pl.program_id(0)
    t_in_seq = i % tiles_per_seq

    @pl.when(i == 0)
    def _():
        cnt_ref[...] = cnt_in_ref[...]

    u = u_ref[...].astype(F32)
    rowi = lax.broadcasted_iota(jnp.int32, (tm, CONV_WIDTH), 0)
    prev_row = jnp.where(t_in_seq == 0, 0.0, up_ref[7:8, :].astype(F32))
    next_row = jnp.where(t_in_seq == tiles_per_seq - 1, 0.0, un_ref[0:1, :].astype(F32))
    u_prev = jnp.where(rowi == 0, prev_row, pltpu.roll(u, 1, 0))
    u_next = jnp.where(rowi == tm - 1, next_row, pltpu.roll(u, tm - 1, 0))
    conv = u_prev * convw_ref[0:1, :] + u * convw_ref[1:2, :] + u_next * convw_ref[2:3, :] + convb_ref[...]
    c_all = _rms(cb_ref[...].astype(F32) * conv, gout_ref[:, ATTN_WIDTH:ATTN_WIDTH + CONV_WIDTH])

    sub_rows = min(MIX_SUB_ROWS, tm)
    counts = cnt_ref[...]
    for r0 in range(0, tm, sub_rows):
        rs = slice(r0, r0 + sub_rows)
        a = _rms(ya_ref[rs, :].astype(F32), gout_ref[:, :ATTN_WIDTH])
        heads = []
        for h in range(N_MEM_HEADS):
            sl = slice(h * HEAD_DIM, (h + 1) * HEAD_DIM)
            s = lax.dot_general(mq_ref[rs, sl], mk_ref[0, :, sl], (((1,), (1,)), ((), ())),
                                preferred_element_type=F32)
            p = jnp.exp(s - jnp.max(s, axis=1, keepdims=True))
            o = jnp.dot(p.astype(BF16), mv_ref[0, :, sl], preferred_element_type=F32)
            heads.append(o / jnp.sum(p, axis=1, keepdims=True))
        ym = _rms(jnp.concatenate(heads, axis=1), gout_ref[:, ATTN_WIDTH + CONV_WIDTH:])

        mixed = jnp.concatenate([a, c_all[rs], ym], axis=1).astype(BF16)
        x1 = x_ref[rs, :] + jnp.dot(mixed, wo_ref[...], preferred_element_type=F32)
        x1_ref[rs, :] = x1
        h2 = _rms(x1, gffn_ref[...])
        h2_ref[rs, :] = h2
        h_hi, h_lo = _split_bf16(h2)
        logits = (jnp.dot(h_hi, wr_hi_ref[...], preferred_element_type=F32)
                  + jnp.dot(h_hi, wr_lo_ref[...], preferred_element_type=F32)
                  + jnp.dot(h_lo, wr_hi_ref[...], preferred_element_type=F32))
        e1, e2, g1, g2 = _route(logits)
        rank1, rank2, counts = _slot_ranks(e1, e2, counts)
        lane = lax.broadcasted_iota(jnp.int32, (sub_rows, ROUTE_LANES), 1)
        info = jnp.zeros((sub_rows, ROUTE_LANES), F32)
        for k, col in enumerate((e1, e2, g1, g2, rank1, rank2)):
            info = jnp.where(lane == k, col, info)
        info_ref[rs, :] = info
    cnt_ref[...] = counts
    cnt_out_ref[...] = counts


def _mix_out(y_attn, u, cb, mq, mk, mv, x2d, seq_len, w_o_b, g_out, conv_w, conv_b, g_ffn,
             wr_hi, wr_lo, tm, counts_in, h2_row0, h2_buf):
    m, d = x2d.shape
    n_mem = mk.shape[1]
    tiles_per_seq = seq_len // tm
    sub = tm // 8
    n_sub = m // 8
    h2_tile0 = h2_row0 // tm
    row = lambda i: (i, 0)
    const = lambda i: (0, 0)
    batch = lambda i: (i // tiles_per_seq, 0, 0)
    h2_buf_arg = 17
    extra_specs, extra_args = [pl.BlockSpec(memory_space=pl.ANY)], [h2_buf]
    return pl.pallas_call(
        functools.partial(_mix_out_kernel, tiles_per_seq=tiles_per_seq),
        out_shape=(jax.ShapeDtypeStruct((m, d), F32), jax.ShapeDtypeStruct(h2_buf.shape, F32),
                   jax.ShapeDtypeStruct((m, ROUTE_LANES), F32),
                   jax.ShapeDtypeStruct((1, ROUTE_LANES), F32)),
        grid=(m // tm,),
        input_output_aliases={h2_buf_arg: 1},
        scratch_shapes=[pltpu.VMEM((1, ROUTE_LANES), F32)],
        in_specs=[
            pl.BlockSpec((tm, ATTN_WIDTH), row),
            pl.BlockSpec((tm, CONV_WIDTH), row),
            pl.BlockSpec((8, CONV_WIDTH), lambda i: (jnp.maximum(i * sub - 1, 0), 0)),
            pl.BlockSpec((8, CONV_WIDTH), lambda i: (jnp.minimum((i + 1) * sub, n_sub - 1), 0)),
            pl.BlockSpec((tm, CONV_WIDTH), row),
            pl.BlockSpec((tm, MEM_WIDTH), row),
            pl.BlockSpec((1, n_mem, MEM_WIDTH), batch),
            pl.BlockSpec((1, n_mem, MEM_WIDTH), batch),
            pl.BlockSpec((tm, d), row),
            pl.BlockSpec((MIX_WIDTH, d), const, pipeline_mode=pl.Buffered(1)),
            pl.BlockSpec((1, MIX_WIDTH), const),
            pl.BlockSpec((3, CONV_WIDTH), const),
            pl.BlockSpec((1, CONV_WIDTH), const),
            pl.BlockSpec((1, d), const),
            pl.BlockSpec((d, ROUTE_LANES), const),
            pl.BlockSpec((d, ROUTE_LANES), const),
            pl.BlockSpec((1, ROUTE_LANES), const),
        ] + extra_specs,
        out_specs=(pl.BlockSpec((tm, d), row), pl.BlockSpec((tm, d), lambda i: (i + h2_tile0, 0)),
                   pl.BlockSpec((tm, ROUTE_LANES), row), pl.BlockSpec((1, ROUTE_LANES), const)),
        compiler_params=_cparams(("arbitrary",)),
        name="mix_out",
    )(y_attn, u, u, u, cb, mq, mk, mv, x2d, w_o_b, g_out.reshape(1, MIX_WIDTH), conv_w,
      conv_b.reshape(1, CONV_WIDTH), g_ffn.reshape(1, d), wr_hi, wr_lo, counts_in, *extra_args)


def _row_copy(src_hbm, idx, dst, row, sem):
    return pltpu.make_async_copy(src_hbm.at[pl.ds(idx, 1), :], dst.at[pl.ds(row, 1), :], sem)


def _experts_kernel(bexp_ref, bfirst_ref, bwslot_ref, bnext_ref, nused_ref, tok_ref,
                    h2_hbm, wg_hbm, wu_hbm, wd_hbm, ys_ref,
                    xbuf, sem, x_b, wg_f, wu_f, wd_f, wsem, wg_b, wu_b, wd_b):
    i = pl.program_id(0)
    n_used = nused_ref[0]
    blk = DISPATCH_BLOCK

    def start_gather(b, slot):
        first = bfirst_ref[b]

        def body(r, carry):
            _row_copy(h2_hbm, tok_ref[first + r], xbuf.at[slot], r, sem.at[slot]).start()
            return carry
        lax.fori_loop(0, blk, body, 0, unroll=DMA_ISSUE_UNROLL)

    def wait_gather(slot):
        pltpu.make_async_copy(h2_hbm.at[pl.ds(0, blk), :], xbuf.at[slot], sem.at[slot]).wait()

    def weight_copies(e, wslot):
        return [pltpu.make_async_copy(src.at[e], dst.at[wslot], wsem.at[wslot])
                for src, dst in ((wg_hbm, wg_f), (wu_hbm, wu_f), (wd_hbm, wd_f))]

    @pl.when(i == 0)
    def _():
        start_gather(0, 0)
        for c in weight_copies(bexp_ref[0], 0):
            c.start()

    @pl.when(i < n_used)
    def _():
        slot = i % 2
        wait_gather(slot)

        @pl.when((i == 0) | (bexp_ref[i] != bexp_ref[jnp.maximum(i - 1, 0)]))
        def _():
            wslot = bwslot_ref[i]
            for c in weight_copies(bexp_ref[i], wslot):
                c.wait()
            wg_b[...] = wg_f[wslot].astype(BF16)
            wu_b[...] = wu_f[wslot].astype(BF16)
            wd_b[...] = wd_f[wslot].astype(BF16)

            @pl.when(bnext_ref[i] >= 0)
            def _():
                for c in weight_copies(bnext_ref[i], 1 - wslot):
                    c.start()

        x_b[...] = xbuf[slot].astype(BF16)
        nxt = bfirst_ref[jnp.minimum(i + 1, n_used - 1)]
        for r in range(blk):
            _row_copy(h2_hbm, tok_ref[nxt + r], xbuf.at[1 - slot], r,
                      sem.at[1 - slot]).start(priority=r % N_DMA_QUEUES)

        x = x_b[...]
        g = jnp.dot(x, wg_b[...], preferred_element_type=F32)
        up = jnp.dot(x, wu_b[...], preferred_element_type=F32)
        hid = (g / (1.0 + jnp.exp(-g))) * up
        ys_ref[...] = jnp.dot(hid.astype(BF16), wd_b[...], preferred_element_type=F32)

        @pl.when(i == n_used - 1)
        def _():
            wait_gather(1 - slot)

    @pl.when(i >= n_used)
    def _():
        ys_ref[...] = jnp.zeros_like(ys_ref)


def _experts(plan, sorted_tokens, h2, w_gate, w_up, w_down):
    block_expert, block_first, block_wslot, block_next, n_used = plan
    n_blocks = block_expert.shape[0]
    d = h2.shape[1]
    blk = DISPATCH_BLOCK
    hbm = pl.BlockSpec(memory_space=pl.ANY)
    grid_spec = pltpu.PrefetchScalarGridSpec(
        num_scalar_prefetch=6,
        grid=(n_blocks,),
        in_specs=[hbm, hbm, hbm, hbm],
        out_specs=pl.BlockSpec((blk, d), lambda i, *_: (i, 0)),
        scratch_shapes=[
            pltpu.VMEM((2, blk, d), F32),
            pltpu.SemaphoreType.DMA((2,)),
            pltpu.VMEM((blk, d), BF16),
            pltpu.VMEM((2, d, D_FF_EXPERT), F32),
            pltpu.VMEM((2, d, D_FF_EXPERT), F32),
            pltpu.VMEM((2, D_FF_EXPERT, d), F32),
            pltpu.SemaphoreType.DMA((2,)),
            pltpu.VMEM((d, D_FF_EXPERT), BF16),
            pltpu.VMEM((d, D_FF_EXPERT), BF16),
            pltpu.VMEM((D_FF_EXPERT, d), BF16),
        ],
    )
    return pl.pallas_call(
        _experts_kernel,
        out_shape=jax.ShapeDtypeStruct((n_blocks * blk, d), F32),
        grid_spec=grid_spec,
        compiler_params=_cparams(("arbitrary",)),
        name="experts",
    )(block_expert, block_first, block_wslot, block_next, n_used, sorted_tokens, h2, w_gate, w_up, w_down)


def _combine_kernel(pos_ref, x1_ref, info_ref, ys_hbm, out_ref, ybuf, sem):
    i = pl.program_id(0)
    n = pl.num_programs(0)
    tm = x1_ref.shape[0]

    def start_gather(t, slot):
        base = t * (tm * TOP_K)
        for r in range(tm):
            for k in range(TOP_K):
                _row_copy(ys_hbm, pos_ref[base + r * TOP_K + k], ybuf.at[slot, k], r,
                          sem.at[slot]).start(priority=r % N_DMA_QUEUES)

    @pl.when(i == 0)
    def _():
        start_gather(0, 0)

    @pl.when(i + 1 < n)
    def _():
        start_gather(i + 1, (i + 1) % 2)

    slot = i % 2
    for k in range(TOP_K):
        pltpu.make_async_copy(ys_hbm.at[pl.ds(0, tm), :], ybuf.at[slot, k], sem.at[slot]).wait()
    info = info_ref[...]
    out_ref[...] = (x1_ref[...] + ybuf[slot, 0] * info[:, 2:3] + ybuf[slot, 1] * info[:, 3:4])


def _combine(pos, x1, info, ys, tm):
    m, d = x1.shape
    grid_spec = pltpu.PrefetchScalarGridSpec(
        num_scalar_prefetch=1,
        grid=(m // tm,),
        in_specs=[
            pl.BlockSpec((tm, d), lambda i, pos: (i, 0)),
            pl.BlockSpec((tm, ROUTE_LANES), lambda i, pos: (i, 0)),
            pl.BlockSpec(memory_space=pl.ANY),
        ],
        out_specs=pl.BlockSpec((tm, d), lambda i, pos: (i, 0)),
        scratch_shapes=[pltpu.VMEM((2, TOP_K, tm, d), F32), pltpu.SemaphoreType.DMA((2,))],
    )
    return pl.pallas_call(
        _combine_kernel,
        out_shape=jax.ShapeDtypeStruct((m, d), F32),
        grid_spec=grid_spec,
        compiler_params=_cparams(("arbitrary",)),
        name="combine",
    )(pos, x1, info, ys)


def _slot_pos_kernel(start_ref, e_ref, rank_ref, pos_ref):
    e = e_ref[...]
    start = jnp.zeros_like(e)
    for k in range(N_EXPERTS):
        start = jnp.where(e == k, start_ref[k], start)
    pos_ref[...] = start + rank_ref[...]


def _slot_positions(padded_start, experts2d, rank2d):
    whole = pl.BlockSpec(experts2d.shape, lambda i, start: (0, 0))
    return pl.pallas_call(
        _slot_pos_kernel,
        out_shape=jax.ShapeDtypeStruct(experts2d.shape, jnp.int32),
        grid_spec=pltpu.PrefetchScalarGridSpec(num_scalar_prefetch=1, grid=(1,), in_specs=[whole, whole],
                                               out_specs=whole),
        name="slot_positions",
    )(padded_start, experts2d, rank2d)


def _dispatch_plan(experts_flat, rank_flat, counts):
    n_slots = experts_flat.shape[0]
    blk = DISPATCH_BLOCK
    n_blocks = -(-n_slots // blk) + N_EXPERTS
    rank = rank_flat
    padded = ((counts + blk - 1) // blk) * blk
    padded_end = jnp.cumsum(padded)
    padded_start = padded_end - padded
    pos = _slot_positions(padded_start.astype(jnp.int32), experts_flat.reshape(-1, LANES),
                          rank.reshape(-1, LANES)).reshape(-1)
    sorted_tokens = jnp.concatenate([jnp.argsort(pos).astype(jnp.int32) // TOP_K,
                                     jnp.zeros((blk,), jnp.int32)])
    n_used = (padded_end[-1] // blk).astype(jnp.int32)
    block_start = jnp.minimum(jnp.arange(n_blocks, dtype=jnp.int32) * blk,
                              jnp.maximum(padded_end[-1] - blk, 0))
    block_expert = jnp.sum(padded_end[None, :] <= block_start[:, None], axis=1)
    block_expert = jnp.minimum(block_expert, N_EXPERTS - 1).astype(jnp.int32)
    start = jnp.cumsum(counts) - counts
    block_first = (block_start - (padded_start - start)[block_expert]).astype(jnp.int32)
    changed = jnp.concatenate([jnp.zeros((1,), jnp.int32),
                               (block_expert[1:] != block_expert[:-1]).astype(jnp.int32)])
    block_wslot = (jnp.cumsum(changed) % 2).astype(jnp.int32)
    none = N_EXPERTS
    ids = jnp.where(counts > 0, jnp.arange(N_EXPERTS, dtype=jnp.int32), none)
    next_incl = lax.cummin(ids, axis=0, reverse=True)
    next_excl = jnp.concatenate([next_incl[1:], jnp.full((1,), none, jnp.int32)])
    block_next = jnp.where(next_excl < none, next_excl, -1)[block_expert].astype(jnp.int32)
    return pos, sorted_tokens, (block_expert, block_first, block_wslot, block_next, n_used.reshape(1))


def _tile(n, pref):
    return pref if n % pref == 0 else n


def kernel(x_prompt, x_sample, mem_prompt, mem_sample, g_mix, w_in, g_q, g_k, conv_w, conv_b, g_mem, w_mem_kv, g_mq, g_mk, g_out, w_o, g_ffn, w_route_group, w_route_expert, w_gate, w_up, w_down):
    assert g_mix.shape[0] == 1, "single layer"
    d = D_MODEL
    w_in_b = w_in[0].astype(BF16)
    w_o_b = w_o[0].astype(BF16)
    w_mem_b = w_mem_kv[0].astype(BF16)
    w_route = jnp.concatenate(
        [w_route_group[0], w_route_expert[0],
         jnp.zeros((d, ROUTE_LANES - N_GROUPS - N_EXPERTS), F32)], axis=1)
    wr_hi, wr_lo = _split_bf16(w_route)

    groups = []
    m_total = x_prompt.shape[0] * x_prompt.shape[1] + x_sample.shape[0] * x_sample.shape[1]
    counts = jnp.zeros((1, ROUTE_LANES), F32)
    h2_all = jnp.zeros((m_total, d), F32)
    row0 = 0
    for x, mem in ((x_prompt, mem_prompt), (x_sample, mem_sample)):
        b, s, _ = x.shape
        tm = _tile(s, 256)
        x2d = x.reshape(b * s, d)
        cos, sin = _rope_tables(s)
        mk, mv = _mem_kv(mem, g_mem[0], w_mem_b, g_mk[0])
        q, k, v, u, cb, mq = _in_proj(x2d, s, g_mix[0], w_in_b, g_q[0], g_k[0], g_mq[0], cos, sin, tm)
        y_attn = _attention(q.reshape(b, s, ATTN_WIDTH), k.reshape(b, s, KV_WIDTH),
                            v.reshape(b, s, KV_WIDTH), _tile(s, 256), _tile(s, 512))
        x1, h2_all, info, counts = _mix_out(
            y_attn.reshape(b * s, ATTN_WIDTH), u, cb, mq, mk, mv, x2d, s, w_o_b, g_out[0], conv_w[0],
            conv_b[0], g_ffn[0], wr_hi, wr_lo, _tile(s, 2 * MIX_SUB_ROWS), counts, row0, h2_all)
        groups.append((x.shape, x1, info, tm))
        row0 += b * s

    route = jnp.concatenate([g[2][:, :6] for g in groups], axis=0)
    experts_flat = route[:, 0:TOP_K].astype(jnp.int32).reshape(-1)
    rank_flat = route[:, 4:4 + TOP_K].astype(jnp.int32).reshape(-1)
    pos, sorted_tokens, plan = _dispatch_plan(
        experts_flat, rank_flat, counts[0, :N_EXPERTS].astype(jnp.int32))
    ys = _experts(plan, sorted_tokens, h2_all, w_gate[0], w_up[0], w_down[0])

    outs = []
    slot0 = 0
    for shape, x1, info, tm in groups:
        m = x1.shape[0]
        out = _combine(pos[slot0:slot0 + m * TOP_K], x1, info, ys, tm)
        outs.append(out.reshape(shape))
        slot0 += m * TOP_K
    return tuple(outs)
```

```python
import functools

import jax
import jax.numpy as jnp
from jax import lax
from jax.experimental import pallas as pl
from jax.experimental.pallas import tpu as pltpu

D_MODEL = 2048
HEAD_DIM = 128
N_Q_HEADS = 8
N_KV_HEADS = 2
Q_PER_KV = N_Q_HEADS // N_KV_HEADS
ATTN_WIDTH = N_Q_HEADS * HEAD_DIM
KV_WIDTH = N_KV_HEADS * HEAD_DIM
CONV_WIDTH = 512
N_MEM_HEADS = 4
MEM_WIDTH = N_MEM_HEADS * HEAD_DIM
MIX_WIDTH = ATTN_WIDTH + CONV_WIDTH + MEM_WIDTH
GRID_W = 64
ROPE_AXIS_DIM = HEAD_DIM // 2
ROPE_THETA = 10000.0
N_GROUPS = 8
EXPERTS_PER_GROUP = 8
N_EXPERTS = N_GROUPS * EXPERTS_PER_GROUP
TOP_K = 2
D_FF_EXPERT = 512
DISPATCH_BLOCK = 256
EPS = 1e-6

LANES = 128
ROUTE_LANES = LANES
V7X_VMEM_LIMIT = 56 * 1024 * 1024

F32 = jnp.float32
BF16 = jnp.bfloat16
NEG_BIG = -1e30
LOG2_E = 1.4426950408889634
DMA_ISSUE_UNROLL = 8
N_DMA_QUEUES = 2
GATHER_AHEAD = 2
N_GATHER_BUFS = GATHER_AHEAD + 1
MIX_SUB_ROWS = 256


def _rms(x, g):
    return x * lax.rsqrt(jnp.mean(x * x, axis=-1, keepdims=True) + EPS) * g


def _cparams(sem, **kw):
    return pltpu.CompilerParams(dimension_semantics=sem, vmem_limit_bytes=V7X_VMEM_LIMIT, **kw)


def _mem_kv_kernel(mem_ref, g_ref, w_ref, gk_ref, mk_ref, mv_ref):
    m = _rms(mem_ref[0], g_ref[...]).astype(BF16)
    kv = jnp.dot(m, w_ref[...], preferred_element_type=F32)
    for h in range(N_MEM_HEADS):
        sl = slice(h * HEAD_DIM, (h + 1) * HEAD_DIM)
        mk_ref[0, :, sl] = _rms(kv[:, sl], gk_ref[...]).astype(BF16)
    mv_ref[0] = kv[:, MEM_WIDTH:].astype(BF16)


def _mem_kv(mem, g_mem, w_mem_kv, g_mk):
    b, n_mem, d = mem.shape
    out = jax.ShapeDtypeStruct((b, n_mem, MEM_WIDTH), BF16)
    return pl.pallas_call(
        _mem_kv_kernel,
        out_shape=(out, out),
        grid=(b,),
        in_specs=[
            pl.BlockSpec((1, n_mem, d), lambda i: (i, 0, 0)),
            pl.BlockSpec((1, d), lambda i: (0, 0)),
            pl.BlockSpec((d, 2 * MEM_WIDTH), lambda i: (0, 0)),
            pl.BlockSpec((1, HEAD_DIM), lambda i: (0, 0)),
        ],
        out_specs=(pl.BlockSpec((1, n_mem, MEM_WIDTH), lambda i: (i, 0, 0)),
                   pl.BlockSpec((1, n_mem, MEM_WIDTH), lambda i: (i, 0, 0))),
        compiler_params=_cparams(("arbitrary",)),
        name="mem_kv",
    )(mem, g_mem.reshape(1, d), w_mem_kv, g_mk.reshape(1, HEAD_DIM))


def _rope_tables(seq_len):
    t = jnp.arange(seq_len, dtype=jnp.int32)
    row = (t // GRID_W).astype(F32)
    col = (t % GRID_W).astype(F32)
    inv_freq = ROPE_THETA ** (-jnp.arange(0, ROPE_AXIS_DIM, 2, dtype=F32) / ROPE_AXIS_DIM)
    ang_r = row[:, None] * inv_freq
    ang_c = col[:, None] * inv_freq
    cos = jnp.concatenate([jnp.cos(ang_r), jnp.cos(ang_r), jnp.cos(ang_c), jnp.cos(ang_c)], axis=-1)
    sin = jnp.concatenate([-jnp.sin(ang_r), jnp.sin(ang_r), -jnp.sin(ang_c), jnp.sin(ang_c)], axis=-1)
    return cos, sin


def _in_proj_kernel(x_ref, gmix_ref, w_ref, gq_ref, gk_ref, gmq_ref, cos_ref, sin_ref,
                    q_ref, k_ref, v_ref, u_ref, cb_ref, mq_ref):
    tm = x_ref.shape[0]
    hb = _rms(x_ref[...], gmix_ref[...]).astype(BF16)
    cos = cos_ref[...]
    sin = sin_ref[...]
    lane = lax.broadcasted_iota(jnp.int32, (tm, HEAD_DIM), 1)
    first_half = (lane % (ROPE_AXIS_DIM)) < (ROPE_AXIS_DIM // 2)
    scale = HEAD_DIM ** -0.5
    q_scale = scale * LOG2_E

    def rope(p):
        swapped = jnp.where(first_half, pltpu.roll(p, HEAD_DIM - ROPE_AXIS_DIM // 2, 1),
                            pltpu.roll(p, ROPE_AXIS_DIM // 2, 1))
        return p * cos + swapped * sin

    def proj(lo, width):
        return jnp.dot(hb, w_ref[:, lo:lo + width], preferred_element_type=F32)

    pq = proj(0, ATTN_WIDTH)
    for h in range(N_Q_HEADS):
        sl = slice(h * HEAD_DIM, (h + 1) * HEAD_DIM)
        q_ref[:, sl] = (rope(_rms(pq[:, sl], gq_ref[...])) * q_scale).astype(BF16)
    pkv = proj(ATTN_WIDTH, 2 * KV_WIDTH)
    for h in range(N_KV_HEADS):
        sl = slice(h * HEAD_DIM, (h + 1) * HEAD_DIM)
        k_ref[:, sl] = rope(_rms(pkv[:, sl], gk_ref[...])).astype(BF16)
    v_ref[...] = pkv[:, KV_WIDTH:].astype(BF16)
    pc = proj(ATTN_WIDTH + 2 * KV_WIDTH, 3 * CONV_WIDTH)
    u_ref[...] = (pc[:, 2 * CONV_WIDTH:] * pc[:, :CONV_WIDTH]).astype(BF16)
    cb_ref[...] = pc[:, CONV_WIDTH:2 * CONV_WIDTH].astype(BF16)
    pm = proj(ATTN_WIDTH + 2 * KV_WIDTH + 3 * CONV_WIDTH, MEM_WIDTH)
    for h in range(N_MEM_HEADS):
        sl = slice(h * HEAD_DIM, (h + 1) * HEAD_DIM)
        mq_ref[:, sl] = (_rms(pm[:, sl], gmq_ref[...]) * scale).astype(BF16)


def _in_proj(x2d, seq_len, g_mix, w_in_b, g_q, g_k, g_mq, cos, sin, tm):
    m, d = x2d.shape
    in_width = w_in_b.shape[1]
    tiles_per_seq = seq_len // tm
    row = lambda i: (i, 0)
    const = lambda i: (0, 0)
    pos = lambda i: (i % tiles_per_seq, 0)
    outs = [(ATTN_WIDTH,), (KV_WIDTH,), (KV_WIDTH,), (CONV_WIDTH,), (CONV_WIDTH,), (MEM_WIDTH,)]
    return pl.pallas_call(
        _in_proj_kernel,
        out_shape=tuple(jax.ShapeDtypeStruct((m, w[0]), BF16) for w in outs),
        grid=(m // tm,),
        in_specs=[
            pl.BlockSpec((tm, d), row),
            pl.BlockSpec((1, d), const),
            pl.BlockSpec((d, in_width), const, pipeline_mode=pl.Buffered(1)),
            pl.BlockSpec((1, HEAD_DIM), const),
            pl.BlockSpec((1, HEAD_DIM), const),
            pl.BlockSpec((1, HEAD_DIM), const),
            pl.BlockSpec((tm, HEAD_DIM), pos),
            pl.BlockSpec((tm, HEAD_DIM), pos),
        ],
        out_specs=tuple(pl.BlockSpec((tm, w[0]), row) for w in outs),
        compiler_params=_cparams(("parallel",)),
        name="in_proj",
    )(x2d, g_mix.reshape(1, d), w_in_b, g_q.reshape(1, HEAD_DIM), g_k.reshape(1, HEAD_DIM),
      g_mq.reshape(1, HEAD_DIM), cos, sin)


def _attn_kernel(q_ref, k_ref, v_ref, o_ref, s_a, s_b, acc_ref, *, tk):
    tq = q_ref.shape[1]
    seq = k_ref.shape[1]
    n = seq // tk
    rows = Q_PER_KV * tq
    q = jnp.concatenate([q_ref[0, :, g * HEAD_DIM:(g + 1) * HEAD_DIM] for g in range(Q_PER_KV)], axis=0)
    ones = jnp.ones((tk, HEAD_DIM), BF16)

    def scores(j, s_ref):
        s = lax.dot_general(q, k_ref[0, j * tk:(j + 1) * tk, :], (((1,), (1,)), ((), ())),
                            preferred_element_type=F32)
        s_ref[...] = s
        return jnp.max(s, axis=1, keepdims=True)

    def update(j, s_ref, mx, m, l):
        m_new = jnp.maximum(m, mx)
        alpha = jnp.exp2(m - m_new)
        p = jnp.exp2(s_ref[...] - m_new).astype(BF16)
        v1 = jnp.concatenate([v_ref[0, j * tk:(j + 1) * tk, :], ones], axis=1)
        pv = jnp.dot(p, v1, preferred_element_type=F32)
        acc_ref[...] = alpha * acc_ref[...] + pv[:, :HEAD_DIM]
        return m_new, alpha * l + pv[:, HEAD_DIM:]

    acc_ref[...] = jnp.zeros_like(acc_ref)
    m0 = jnp.full((rows, 1), NEG_BIG, F32)
    l0 = jnp.zeros((rows, HEAD_DIM), F32)
    bufs = (s_a, s_b)
    m, l = m0, l0
    mx = scores(0, bufs[0])
    for j in range(n):
        mx_next = scores(j + 1, bufs[(j + 1) % 2]) if j + 1 < n else None
        m, l = update(j, bufs[j % 2], mx, m, l)
        mx = mx_next
    o = acc_ref[...] / l
    for g in range(Q_PER_KV):
        o_ref[0, :, g * HEAD_DIM:(g + 1) * HEAD_DIM] = o[g * tq:(g + 1) * tq].astype(BF16)


def _attention(q, k, v, tq, tk):
    b, s, _ = q.shape
    grp_w = Q_PER_KV * HEAD_DIM
    rows = Q_PER_KV * tq
    return pl.pallas_call(
        functools.partial(_attn_kernel, tk=tk),
        out_shape=jax.ShapeDtypeStruct((b, s, ATTN_WIDTH), BF16),
        grid=(b, N_KV_HEADS, s // tq),
        in_specs=[
            pl.BlockSpec((1, tq, grp_w), lambda bi, h, qi: (bi, qi, h)),
            pl.BlockSpec((1, s, HEAD_DIM), lambda bi, h, qi: (bi, 0, h)),
            pl.BlockSpec((1, s, HEAD_DIM), lambda bi, h, qi: (bi, 0, h)),
        ],
        out_specs=pl.BlockSpec((1, tq, grp_w), lambda bi, h, qi: (bi, qi, h)),
        scratch_shapes=[pltpu.VMEM((rows, tk), F32), pltpu.VMEM((rows, tk), F32),
                        pltpu.VMEM((rows, HEAD_DIM), F32)],
        compiler_params=_cparams(("parallel", "parallel", "parallel")),
        name="attention",
    )(q, k, v)


def _route(logits):
    tm = logits.shape[0]
    lane = lax.broadcasted_iota(jnp.int32, (tm, ROUTE_LANES), 1)
    lane_f = lane.astype(F32)
    no_lane = float(ROUTE_LANES)
    is_g = lane < N_GROUPS
    gl = jnp.where(is_g, logits, NEG_BIG)
    gmax = jnp.max(gl, axis=1, keepdims=True)
    gsum = jnp.sum(jnp.where(is_g, jnp.exp(gl - gmax), 0.0), axis=1, keepdims=True)
    p_group = 1.0 / gsum
    gidx = jnp.min(jnp.where(is_g & (gl == gmax), lane_f, no_lane), axis=1, keepdims=True)
    lo = N_GROUPS + EXPERTS_PER_GROUP * gidx
    in_grp = (lane_f >= lo) & (lane_f < lo + EXPERTS_PER_GROUP)
    el = jnp.where(in_grp, logits, NEG_BIG)
    emax = jnp.max(el, axis=1, keepdims=True)
    j1 = jnp.min(jnp.where(in_grp & (el == emax), lane_f, no_lane), axis=1, keepdims=True)
    rest = in_grp & (lane_f != j1)
    el2 = jnp.where(rest, logits, NEG_BIG)
    e2max = jnp.max(el2, axis=1, keepdims=True)
    j2 = jnp.min(jnp.where(rest & (el2 == e2max), lane_f, no_lane), axis=1, keepdims=True)
    r = jnp.exp(e2max - emax)
    g1 = p_group / (1.0 + r)
    g2 = g1 * r
    return j1 - N_GROUPS, j2 - N_GROUPS, g1, g2


def _slot_ranks(e1, e2, counts):
    tm = e1.shape[0]
    lane_f = lax.broadcasted_iota(jnp.int32, (tm, ROUTE_LANES), 1).astype(F32)
    oh1 = lane_f == e1
    oh2 = lane_f == e2
    both = jnp.where(oh1 | oh2, 1.0, 0.0)
    earlier = (lax.broadcasted_iota(jnp.int32, (tm, tm), 1)
               < lax.broadcasted_iota(jnp.int32, (tm, tm), 0))
    before = jnp.dot(jnp.where(earlier, 1.0, 0.0).astype(BF16), both.astype(BF16),
                     preferred_element_type=F32) + counts
    rank1 = jnp.sum(jnp.where(oh1, before, 0.0), axis=1, keepdims=True)
    rank2 = jnp.sum(jnp.where(oh2, before, 0.0), axis=1, keepdims=True)
    return rank1, rank2, counts + jnp.sum(both, axis=0, keepdims=True)


def _split_bf16(a):
    hi = a.astype(BF16)
    lo = (a - hi.astype(F32)).astype(BF16)
    return hi, lo


def _mix_out_kernel(ya_ref, u_ref, up_ref, un_ref, cb_ref, mq_ref, mk_ref, mv_ref, x_ref,
                    wo_ref, gout_ref, convw_ref, convb_ref, gffn_ref, wr_hi_ref, wr_lo_ref, cnt_in_ref,
                    h2_buf_ref, x1_ref, h2_ref, info_ref, cnt_out_ref, cnt_ref, *, tiles_per_seq):
    del h2_buf_ref
    tm = x_ref.shape[0]
    i = pl.program_id(0)
    t_in_seq = i % tiles_per_seq

    @pl.when(i == 0)
    def _():
        cnt_ref[...] = cnt_in_ref[...]

    u = u_ref[...].astype(F32)
    rowi = lax.broadcasted_iota(jnp.int32, (tm, CONV_WIDTH), 0)
    prev_row = jnp.where(t_in_seq == 0, 0.0, up_ref[7:8, :].astype(F32))
    next_row = jnp.where(t_in_seq == tiles_per_seq - 1, 0.0, un_ref[0:1, :].astype(F32))
    u_prev = jnp.where(rowi == 0, prev_row, pltpu.roll(u, 1, 0))
    u_next = jnp.where(rowi == tm - 1, next_row, pltpu.roll(u, tm - 1, 0))
    conv = u_prev * convw_ref[0:1, :] + u * convw_ref[1:2, :] + u_next * convw_ref[2:3, :] + convb_ref[...]
    c_all = _rms(cb_ref[...].astype(F32) * conv, gout_ref[:, ATTN_WIDTH:ATTN_WIDTH + CONV_WIDTH])

    sub_rows = min(MIX_SUB_ROWS, tm)
    counts = cnt_ref[...]
    for r0 in range(0, tm, sub_rows):
        rs = slice(r0, r0 + sub_rows)
        a = _rms(ya_ref[rs, :].astype(F32), gout_ref[:, :ATTN_WIDTH])
        heads = []
        for h in range(N_MEM_HEADS):
            sl = slice(h * HEAD_DIM, (h + 1) * HEAD_DIM)
            s = lax.dot_general(mq_ref[rs, sl], mk_ref[0, :, sl], (((1,), (1,)), ((), ())),
                                preferred_element_type=F32)
            p = jnp.exp(s - jnp.max(s, axis=1, keepdims=True))
            o = jnp.dot(p.astype(BF16), mv_ref[0, :, sl], preferred_element_type=F32)
            heads.append(o / jnp.sum(p, axis=1, keepdims=True))
        ym = _rms(jnp.concatenate(heads, axis=1), gout_ref[:, ATTN_WIDTH + CONV_WIDTH:])

        mixed = jnp.concatenate([a, c_all[rs], ym], axis=1).astype(BF16)
        x1 = x_ref[rs, :] + jnp.dot(mixed, wo_ref[...], preferred_element_type=F32)
        x1_ref[rs, :] = x1
        h2 = _rms(x1, gffn_ref[...])
        h2_ref[rs, :] = h2
        h_hi, h_lo = _split_bf16(h2)
        logits = (jnp.dot(h_hi, wr_hi_ref[...], preferred_element_type=F32)
                  + jnp.dot(h_hi, wr_lo_ref[...], preferred_element_type=F32)
                  + jnp.dot(h_lo, wr_hi_ref[...], preferred_element_type=F32))
        e1, e2, g1, g2 = _route(logits)
        rank1, rank2, counts = _slot_ranks(e1, e2, counts)
        lane = lax.broadcasted_iota(jnp.int32, (sub_rows, ROUTE_LANES), 1)
        info = jnp.zeros((sub_rows, ROUTE_LANES), F32)
        for k, col in enumerate((e1, e2, g1, g2, rank1, rank2)):
            info = jnp.where(lane == k, col, info)
        info_ref[rs, :] = info
    cnt_ref[...] = counts
    cnt_out_ref[...] = counts


def _mix_out(y_attn, u, cb, mq, mk, mv, x2d, seq_len, w_o_b, g_out, conv_w, conv_b, g_ffn,
             wr_hi, wr_lo, tm, counts_in, h2_row0, h2_buf):
    m, d = x2d.shape
    n_mem = mk.shape[1]
    tiles_per_seq = seq_len // tm
    sub = tm // 8
    n_sub = m // 8
    h2_tile0 = h2_row0 // tm
    row = lambda i: (i, 0)
    const = lambda i: (0, 0)
    batch = lambda i: (i // tiles_per_seq, 0, 0)
    h2_buf_arg = 17
    extra_specs, extra_args = [pl.BlockSpec(memory_space=pl.ANY)], [h2_buf]
    return pl.pallas_call(
        functools.partial(_mix_out_kernel, tiles_per_seq=tiles_per_seq),
        out_shape=(jax.ShapeDtypeStruct((m, d), F32), jax.ShapeDtypeStruct(h2_buf.shape, F32),
                   jax.ShapeDtypeStruct((m, ROUTE_LANES), F32),
                   jax.ShapeDtypeStruct((1, ROUTE_LANES), F32)),
        grid=(m // tm,),
        input_output_aliases={h2_buf_arg: 1},
        scratch_shapes=[pltpu.VMEM((1, ROUTE_LANES), F32)],
        in_specs=[
            pl.BlockSpec((tm, ATTN_WIDTH), row),
            pl.BlockSpec((tm, CONV_WIDTH), row),
            pl.BlockSpec((8, CONV_WIDTH), lambda i: (jnp.maximum(i * sub - 1, 0), 0)),
            pl.BlockSpec((8, CONV_WIDTH), lambda i: (jnp.minimum((i + 1) * sub, n_sub - 1), 0)),
            pl.BlockSpec((tm, CONV_WIDTH), row),
            pl.BlockSpec((tm, MEM_WIDTH), row),
            pl.BlockSpec((1, n_mem, MEM_WIDTH), batch),
            pl.BlockSpec((1, n_mem, MEM_WIDTH), batch),
            pl.BlockSpec((tm, d), row),
            pl.BlockSpec((MIX_WIDTH, d), const, pipeline_mode=pl.Buffered(1)),
            pl.BlockSpec((1, MIX_WIDTH), const),
            pl.BlockSpec((3, CONV_WIDTH), const),
            pl.BlockSpec((1, CONV_WIDTH), const),
            pl.BlockSpec((1, d), const),
            pl.BlockSpec((d, ROUTE_LANES), const),
            pl.BlockSpec((d, ROUTE_LANES), const),
            pl.BlockSpec((1, ROUTE_LANES), const),
        ] + extra_specs,
        out_specs=(pl.BlockSpec((tm, d), row), pl.BlockSpec((tm, d), lambda i: (i + h2_tile0, 0)),
                   pl.BlockSpec((tm, ROUTE_LANES), row), pl.BlockSpec((1, ROUTE_LANES), const)),
        compiler_params=_cparams(("arbitrary",)),
        name="mix_out",
    )(y_attn, u, u, u, cb, mq, mk, mv, x2d, w_o_b, g_out.reshape(1, MIX_WIDTH), conv_w,
      conv_b.reshape(1, CONV_WIDTH), g_ffn.reshape(1, d), wr_hi, wr_lo, counts_in, *extra_args)


def _row_copy(src_hbm, idx, dst, row, sem):
    return pltpu.make_async_copy(src_hbm.at[pl.ds(idx, 1), :], dst.at[pl.ds(row, 1), :], sem)


def _experts_kernel(bexp_ref, bfirst_ref, bwslot_ref, bnext_ref, nused_ref, tok_ref,
                    h2_hbm, wg_hbm, wu_hbm, wd_hbm, ys_ref,
                    xbuf, sem, x_b, wg_f, wu_f, wd_f, wsem, wg_b, wu_b, wd_b):
    i = pl.program_id(0)
    n_used = nused_ref[0]
    blk = DISPATCH_BLOCK

    def start_gather(b, slot):
        first = bfirst_ref[b]

        def body(r, carry):
            _row_copy(h2_hbm, tok_ref[first + r], xbuf.at[slot], r, sem.at[slot]).start()
            return carry
        lax.fori_loop(0, blk, body, 0, unroll=DMA_ISSUE_UNROLL)

    def wait_gather(slot):
        pltpu.make_async_copy(h2_hbm.at[pl.ds(0, blk), :], xbuf.at[slot], sem.at[slot]).wait()

    def weight_copies(e, wslot):
        return [pltpu.make_async_copy(src.at[e], dst.at[wslot], wsem.at[wslot])
                for src, dst in ((wg_hbm, wg_f), (wu_hbm, wu_f), (wd_hbm, wd_f))]

    @pl.when(i == 0)
    def _():
        for ahead in range(GATHER_AHEAD):
            start_gather(jnp.minimum(ahead, n_used - 1), ahead)
        for c in weight_copies(bexp_ref[0], 0):
            c.start()

    @pl.when(i < n_used)
    def _():
        slot = i % N_GATHER_BUFS
        wait_gather(slot)

        @pl.when((i == 0) | (bexp_ref[i] != bexp_ref[jnp.maximum(i - 1, 0)]))
        def _():
            wslot = bwslot_ref[i]
            for c in weight_copies(bexp_ref[i], wslot):
                c.wait()
            wg_b[...] = wg_f[wslot].astype(BF16)
            wu_b[...] = wu_f[wslot].astype(BF16)
            wd_b[...] = wd_f[wslot].astype(BF16)

            @pl.when(bnext_ref[i] >= 0)
            def _():
                for c in weight_copies(bnext_ref[i], 1 - wslot):
                    c.start()

        x_b[...] = xbuf[slot].astype(BF16)
        nslot = (i + GATHER_AHEAD) % N_GATHER_BUFS
        nxt = bfirst_ref[jnp.minimum(i + GATHER_AHEAD, n_used - 1)]
        for r in range(blk):
            _row_copy(h2_hbm, tok_ref[nxt + r], xbuf.at[nslot], r,
                      sem.at[nslot]).start(priority=r % N_DMA_QUEUES)

        x = x_b[...]
        g = jnp.dot(x, wg_b[...], preferred_element_type=F32)
        up = jnp.dot(x, wu_b[...], preferred_element_type=F32)
        hid = (g / (1.0 + jnp.exp(-g))) * up
        ys_ref[...] = jnp.dot(hid.astype(BF16), wd_b[...], preferred_element_type=F32)

        @pl.when(i == n_used - 1)
        def _():
            for ahead in range(1, GATHER_AHEAD + 1):
                wait_gather((i + ahead) % N_GATHER_BUFS)

    @pl.when(i >= n_used)
    def _():
        ys_ref[...] = jnp.zeros_like(ys_ref)


def _experts(plan, sorted_tokens, h2, w_gate, w_up, w_down):
    block_expert, block_first, block_wslot, block_next, n_used = plan
    n_blocks = block_expert.shape[0]
    d = h2.shape[1]
    blk = DISPATCH_BLOCK
    hbm = pl.BlockSpec(memory_space=pl.ANY)
    grid_spec = pltpu.PrefetchScalarGridSpec(
        num_scalar_prefetch=6,
        grid=(n_blocks,),
        in_specs=[hbm, hbm, hbm, hbm],
        out_specs=pl.BlockSpec((blk, d), lambda i, *_: (i, 0)),
        scratch_shapes=[
            pltpu.VMEM((N_GATHER_BUFS, blk, d), F32),
            pltpu.SemaphoreType.DMA((N_GATHER_BUFS,)),
            pltpu.VMEM((blk, d), BF16),
            pltpu.VMEM((2, d, D_FF_EXPERT), F32),
            pltpu.VMEM((2, d, D_FF_EXPERT), F32),
            pltpu.VMEM((2, D_FF_EXPERT, d), F32),
            pltpu.SemaphoreType.DMA((2,)),
            pltpu.VMEM((d, D_FF_EXPERT), BF16),
            pltpu.VMEM((d, D_FF_EXPERT), BF16),
            pltpu.VMEM((D_FF_EXPERT, d), BF16),
        ],
    )
    return pl.pallas_call(
        _experts_kernel,
        out_shape=jax.ShapeDtypeStruct((n_blocks * blk, d), F32),
        grid_spec=grid_spec,
        compiler_params=_cparams(("arbitrary",)),
        name="experts",
    )(block_expert, block_first, block_wslot, block_next, n_used, sorted_tokens, h2, w_gate, w_up, w_down)


def _combine_kernel(pos_ref, x1_ref, info_ref, ys_hbm, out_ref, ybuf, sem, *, n):
    i = pl.program_id(0)
    tm = x1_ref.shape[0]

    def start_gather(t, slot):
        base = t * (tm * TOP_K)
        for r in range(tm):
            for k in range(TOP_K):
                _row_copy(ys_hbm, pos_ref[base + r * TOP_K + k], ybuf.at[slot, k], r,
                          sem.at[slot]).start(priority=r % N_DMA_QUEUES)

    def start_gather_rolled(t, slot):
        def body(r, carry):
            for k in range(TOP_K):
                _row_copy(ys_hbm, pos_ref[(t * tm + r) * TOP_K + k], ybuf.at[slot, k], r,
                          sem.at[slot]).start()
            return carry
        lax.fori_loop(0, tm, body, 0, unroll=DMA_ISSUE_UNROLL // TOP_K)

    @pl.when(i == 0)
    def _():
        for ahead in range(min(GATHER_AHEAD, n)):
            start_gather_rolled(ahead, ahead)

    @pl.when(i + GATHER_AHEAD < n)
    def _():
        start_gather(i + GATHER_AHEAD, (i + GATHER_AHEAD) % N_GATHER_BUFS)

    slot = i % N_GATHER_BUFS
    for k in range(TOP_K):
        pltpu.make_async_copy(ys_hbm.at[pl.ds(0, tm), :], ybuf.at[slot, k], sem.at[slot]).wait()
    info = info_ref[...]
    out_ref[...] = (x1_ref[...] + ybuf[slot, 0] * info[:, 2:3] + ybuf[slot, 1] * info[:, 3:4])


def _combine(pos, x1, info, ys, tm):
    m, d = x1.shape
    grid_spec = pltpu.PrefetchScalarGridSpec(
        num_scalar_prefetch=1,
        grid=(m // tm,),
        in_specs=[
            pl.BlockSpec((tm, d), lambda i, pos: (i, 0)),
            pl.BlockSpec((tm, ROUTE_LANES), lambda i, pos: (i, 0)),
            pl.BlockSpec(memory_space=pl.ANY),
        ],
        out_specs=pl.BlockSpec((tm, d), lambda i, pos: (i, 0)),
        scratch_shapes=[pltpu.VMEM((N_GATHER_BUFS, TOP_K, tm, d), F32),
                        pltpu.SemaphoreType.DMA((N_GATHER_BUFS,))],
    )
    return pl.pallas_call(
        functools.partial(_combine_kernel, n=m // tm),
        out_shape=jax.ShapeDtypeStruct((m, d), F32),
        grid_spec=grid_spec,
        compiler_params=_cparams(("arbitrary",)),
        name="combine",
    )(pos, x1, info, ys)


def _slot_pos_kernel(start_ref, e_ref, rank_ref, pos_ref):
    e = e_ref[...]
    start = jnp.zeros_like(e)
    for k in range(N_EXPERTS):
        start = jnp.where(e == k, start_ref[k], start)
    pos_ref[...] = start + rank_ref[...]


def _slot_positions(padded_start, experts2d, rank2d):
    whole = pl.BlockSpec(experts2d.shape, lambda i, start: (0, 0))
    return pl.pallas_call(
        _slot_pos_kernel,
        out_shape=jax.ShapeDtypeStruct(experts2d.shape, jnp.int32),
        grid_spec=pltpu.PrefetchScalarGridSpec(num_scalar_prefetch=1, grid=(1,), in_specs=[whole, whole],
                                               out_specs=whole),
        name="slot_positions",
    )(padded_start, experts2d, rank2d)


def _dispatch_plan(experts_flat, rank_flat, counts):
    n_slots = experts_flat.shape[0]
    blk = DISPATCH_BLOCK
    n_blocks = -(-n_slots // blk) + N_EXPERTS
    rank = rank_flat
    padded = ((counts + blk - 1) // blk) * blk
    padded_end = jnp.cumsum(padded)
    padded_start = padded_end - padded
    pos = _slot_positions(padded_start.astype(jnp.int32), experts_flat.reshape(-1, LANES),
                          rank.reshape(-1, LANES)).reshape(-1)
    sorted_tokens = jnp.concatenate([jnp.argsort(pos).astype(jnp.int32) // TOP_K,
                                     jnp.zeros((blk,), jnp.int32)])
    n_used = (padded_end[-1] // blk).astype(jnp.int32)
    block_start = jnp.minimum(jnp.arange(n_blocks, dtype=jnp.int32) * blk,
                              jnp.maximum(padded_end[-1] - blk, 0))
    block_expert = jnp.sum(padded_end[None, :] <= block_start[:, None], axis=1)
    block_expert = jnp.minimum(block_expert, N_EXPERTS - 1).astype(jnp.int32)
    start = jnp.cumsum(counts) - counts
    block_first = (block_start - (padded_start - start)[block_expert]).astype(jnp.int32)
    changed = jnp.concatenate([jnp.zeros((1,), jnp.int32),
                               (block_expert[1:] != block_expert[:-1]).astype(jnp.int32)])
    block_wslot = (jnp.cumsum(changed) % 2).astype(jnp.int32)
    none = N_EXPERTS
    ids = jnp.where(counts > 0, jnp.arange(N_EXPERTS, dtype=jnp.int32), none)
    next_incl = lax.cummin(ids, axis=0, reverse=True)
    next_excl = jnp.concatenate([next_incl[1:], jnp.full((1,), none, jnp.int32)])
    block_next = jnp.where(next_excl < none, next_excl, -1)[block_expert].astype(jnp.int32)
    return pos, sorted_tokens, (block_expert, block_first, block_wslot, block_next, n_used.reshape(1))


def _tile(n, pref):
    return pref if n % pref == 0 else n


def kernel(x_prompt, x_sample, mem_prompt, mem_sample, g_mix, w_in, g_q, g_k, conv_w, conv_b, g_mem, w_mem_kv, g_mq, g_mk, g_out, w_o, g_ffn, w_route_group, w_route_expert, w_gate, w_up, w_down):
    assert g_mix.shape[0] == 1, "single layer"
    d = D_MODEL
    w_in_b = w_in[0].astype(BF16)
    w_o_b = w_o[0].astype(BF16)
    w_mem_b = w_mem_kv[0].astype(BF16)
    w_route = jnp.concatenate(
        [w_route_group[0], w_route_expert[0],
         jnp.zeros((d, ROUTE_LANES - N_GROUPS - N_EXPERTS), F32)], axis=1)
    wr_hi, wr_lo = _split_bf16(w_route)

    groups = []
    m_total = x_prompt.shape[0] * x_prompt.shape[1] + x_sample.shape[0] * x_sample.shape[1]
    counts = jnp.zeros((1, ROUTE_LANES), F32)
    h2_all = jnp.zeros((m_total, d), F32)
    row0 = 0
    for x, mem in ((x_prompt, mem_prompt), (x_sample, mem_sample)):
        b, s, _ = x.shape
        tm = _tile(s, 256)
        x2d = x.reshape(b * s, d)
        cos, sin = _rope_tables(s)
        mk, mv = _mem_kv(mem, g_mem[0], w_mem_b, g_mk[0])
        q, k, v, u, cb, mq = _in_proj(x2d, s, g_mix[0], w_in_b, g_q[0], g_k[0], g_mq[0], cos, sin, tm)
        y_attn = _attention(q.reshape(b, s, ATTN_WIDTH), k.reshape(b, s, KV_WIDTH),
                            v.reshape(b, s, KV_WIDTH), _tile(s, 256), _tile(s, 512))
        x1, h2_all, info, counts = _mix_out(
            y_attn.reshape(b * s, ATTN_WIDTH), u, cb, mq, mk, mv, x2d, s, w_o_b, g_out[0], conv_w[0],
            conv_b[0], g_ffn[0], wr_hi, wr_lo, _tile(s, 2 * MIX_SUB_ROWS), counts, row0, h2_all)
        groups.append((x.shape, x1, info, tm))
        row0 += b * s

    route = jnp.concatenate([g[2][:, :6] for g in groups], axis=0)
    experts_flat = route[:, 0:TOP_K].astype(jnp.int32).reshape(-1)
    rank_flat = route[:, 4:4 + TOP_K].astype(jnp.int32).reshape(-1)
    pos, sorted_tokens, plan = _dispatch_plan(
        experts_flat, rank_flat, counts[0, :N_EXPERTS].astype(jnp.int32))
    ys = _experts(plan, sorted_tokens, h2_all, w_gate[0], w_up[0], w_down[0])

    outs = []
    slot0 = 0
    for shape, x1, info, tm in groups:
        m = x1.shape[0]
        out = _combine(pos[slot0:slot0 + m * TOP_K], x1, info, ys, tm)
        outs.append(out.reshape(shape))
        slot0 += m * TOP_K
    return tuple(outs)
```

```python
import functools

import jax
import jax.numpy as jnp
from jax import lax
from jax.experimental import pallas as pl
from jax.experimental.pallas import tpu as pltpu

D_MODEL = 2048
HEAD_DIM = 128
N_Q_HEADS = 8
N_KV_HEADS = 2
Q_PER_KV = N_Q_HEADS // N_KV_HEADS
ATTN_WIDTH = N_Q_HEADS * HEAD_DIM
KV_WIDTH = N_KV_HEADS * HEAD_DIM
CONV_WIDTH = 512
N_MEM_HEADS = 4
MEM_WIDTH = N_MEM_HEADS * HEAD_DIM
MIX_WIDTH = ATTN_WIDTH + CONV_WIDTH + MEM_WIDTH
GRID_W = 64
ROPE_AXIS_DIM = HEAD_DIM // 2
ROPE_THETA = 10000.0
N_GROUPS = 8
EXPERTS_PER_GROUP = 8
N_EXPERTS = N_GROUPS * EXPERTS_PER_GROUP
TOP_K = 2
D_FF_EXPERT = 512
DISPATCH_BLOCK = 256
EPS = 1e-6

LANES = 128
ROUTE_LANES = LANES
ROUTE_ROWS = 8
V7X_VMEM_LIMIT = 56 * 1024 * 1024

F32 = jnp.float32
BF16 = jnp.bfloat16
NEG_BIG = -1e30
LOG2_E = 1.4426950408889634
DMA_ISSUE_UNROLL = 8
N_DMA_QUEUES = 2
GATHER_AHEAD = 2
N_GATHER_BUFS = GATHER_AHEAD + 1
MIX_SUB_ROWS = 256


def _rms(x, g):
    return x * lax.rsqrt(jnp.mean(x * x, axis=-1, keepdims=True) + EPS) * g


def _cparams(sem, **kw):
    return pltpu.CompilerParams(dimension_semantics=sem, vmem_limit_bytes=V7X_VMEM_LIMIT, **kw)


def _mem_kv_kernel(mem_ref, g_ref, w_ref, gk_ref, mk_ref, mv_ref):
    m = _rms(mem_ref[0], g_ref[...]).astype(BF16)
    kv = jnp.dot(m, w_ref[...], preferred_element_type=F32)
    for h in range(N_MEM_HEADS):
        sl = slice(h * HEAD_DIM, (h + 1) * HEAD_DIM)
        mk_ref[0, :, sl] = _rms(kv[:, sl], gk_ref[...]).astype(BF16)
    mv_ref[0] = kv[:, MEM_WIDTH:].astype(BF16)


def _mem_kv(mem, g_mem, w_mem_kv, g_mk):
    b, n_mem, d = mem.shape
    out = jax.ShapeDtypeStruct((b, n_mem, MEM_WIDTH), BF16)
    return pl.pallas_call(
        _mem_kv_kernel,
        out_shape=(out, out),
        grid=(b,),
        in_specs=[
            pl.BlockSpec((1, n_mem, d), lambda i: (i, 0, 0)),
            pl.BlockSpec((1, d), lambda i: (0, 0)),
            pl.BlockSpec((d, 2 * MEM_WIDTH), lambda i: (0, 0)),
            pl.BlockSpec((1, HEAD_DIM), lambda i: (0, 0)),
        ],
        out_specs=(pl.BlockSpec((1, n_mem, MEM_WIDTH), lambda i: (i, 0, 0)),
                   pl.BlockSpec((1, n_mem, MEM_WIDTH), lambda i: (i, 0, 0))),
        compiler_params=_cparams(("arbitrary",)),
        name="mem_kv",
    )(mem, g_mem.reshape(1, d), w_mem_kv, g_mk.reshape(1, HEAD_DIM))


def _rope_tables(seq_len):
    t = jnp.arange(seq_len, dtype=jnp.int32)
    row = (t // GRID_W).astype(F32)
    col = (t % GRID_W).astype(F32)
    inv_freq = ROPE_THETA ** (-jnp.arange(0, ROPE_AXIS_DIM, 2, dtype=F32) / ROPE_AXIS_DIM)
    ang_r = row[:, None] * inv_freq
    ang_c = col[:, None] * inv_freq
    cos = jnp.concatenate([jnp.cos(ang_r), jnp.cos(ang_r), jnp.cos(ang_c), jnp.cos(ang_c)], axis=-1)
    sin = jnp.concatenate([-jnp.sin(ang_r), jnp.sin(ang_r), -jnp.sin(ang_c), jnp.sin(ang_c)], axis=-1)
    return cos, sin


def _in_proj_kernel(x_ref, gmix_ref, w_ref, gq_ref, gk_ref, gmq_ref, cos_ref, sin_ref,
                    q_ref, k_ref, v_ref, u_ref, cb_ref, mq_ref):
    tm = x_ref.shape[0]
    hb = _rms(x_ref[...], gmix_ref[...]).astype(BF16)
    cos = cos_ref[...]
    sin = sin_ref[...]
    lane = lax.broadcasted_iota(jnp.int32, (tm, HEAD_DIM), 1)
    first_half = (lane % (ROPE_AXIS_DIM)) < (ROPE_AXIS_DIM // 2)
    scale = HEAD_DIM ** -0.5
    q_scale = scale * LOG2_E

    def rope(p):
        swapped = jnp.where(first_half, pltpu.roll(p, HEAD_DIM - ROPE_AXIS_DIM // 2, 1),
                            pltpu.roll(p, ROPE_AXIS_DIM // 2, 1))
        return p * cos + swapped * sin

    def proj(lo, width):
        return jnp.dot(hb, w_ref[:, lo:lo + width], preferred_element_type=F32)

    pq = proj(0, ATTN_WIDTH)
    for h in range(N_Q_HEADS):
        sl = slice(h * HEAD_DIM, (h + 1) * HEAD_DIM)
        q_ref[:, sl] = (rope(_rms(pq[:, sl], gq_ref[...])) * q_scale).astype(BF16)
    pkv = proj(ATTN_WIDTH, 2 * KV_WIDTH)
    for h in range(N_KV_HEADS):
        sl = slice(h * HEAD_DIM, (h + 1) * HEAD_DIM)
        k_ref[:, sl] = rope(_rms(pkv[:, sl], gk_ref[...])).astype(BF16)
    v_ref[...] = pkv[:, KV_WIDTH:].astype(BF16)
    pc = proj(ATTN_WIDTH + 2 * KV_WIDTH, 3 * CONV_WIDTH)
    u_ref[...] = (pc[:, 2 * CONV_WIDTH:] * pc[:, :CONV_WIDTH]).astype(BF16)
    cb_ref[...] = pc[:, CONV_WIDTH:2 * CONV_WIDTH].astype(BF16)
    pm = proj(ATTN_WIDTH + 2 * KV_WIDTH + 3 * CONV_WIDTH, MEM_WIDTH)
    for h in range(N_MEM_HEADS):
        sl = slice(h * HEAD_DIM, (h + 1) * HEAD_DIM)
        mq_ref[:, sl] = (_rms(pm[:, sl], gmq_ref[...]) * scale).astype(BF16)


def _in_proj(x2d, seq_len, g_mix, w_in_b, g_q, g_k, g_mq, cos, sin, tm):
    m, d = x2d.shape
    in_width = w_in_b.shape[1]
    tiles_per_seq = seq_len // tm
    row = lambda i: (i, 0)
    const = lambda i: (0, 0)
    pos = lambda i: (i % tiles_per_seq, 0)
    outs = [(ATTN_WIDTH,), (KV_WIDTH,), (KV_WIDTH,), (CONV_WIDTH,), (CONV_WIDTH,), (MEM_WIDTH,)]
    return pl.pallas_call(
        _in_proj_kernel,
        out_shape=tuple(jax.ShapeDtypeStruct((m, w[0]), BF16) for w in outs),
        grid=(m // tm,),
        in_specs=[
            pl.BlockSpec((tm, d), row),
            pl.BlockSpec((1, d), const),
            pl.BlockSpec((d, in_width), const, pipeline_mode=pl.Buffered(1)),
            pl.BlockSpec((1, HEAD_DIM), const),
            pl.BlockSpec((1, HEAD_DIM), const),
            pl.BlockSpec((1, HEAD_DIM), const),
            pl.BlockSpec((tm, HEAD_DIM), pos),
            pl.BlockSpec((tm, HEAD_DIM), pos),
        ],
        out_specs=tuple(pl.BlockSpec((tm, w[0]), row) for w in outs),
        compiler_params=_cparams(("parallel",)),
        name="in_proj",
    )(x2d, g_mix.reshape(1, d), w_in_b, g_q.reshape(1, HEAD_DIM), g_k.reshape(1, HEAD_DIM),
      g_mq.reshape(1, HEAD_DIM), cos, sin)


def _attn_kernel(q_ref, k_ref, v_ref, o_ref, s_a, s_b, acc_ref, *, tk):
    tq = q_ref.shape[1]
    seq = k_ref.shape[1]
    n = seq // tk
    rows = Q_PER_KV * tq
    q = jnp.concatenate([q_ref[0, :, g * HEAD_DIM:(g + 1) * HEAD_DIM] for g in range(Q_PER_KV)], axis=0)
    ones = jnp.ones((tk, HEAD_DIM), BF16)

    def scores(j, s_ref):
        s = lax.dot_general(q, k_ref[0, j * tk:(j + 1) * tk, :], (((1,), (1,)), ((), ())),
                            preferred_element_type=F32)
        s_ref[...] = s
        return jnp.max(s, axis=1, keepdims=True)

    def update(j, s_ref, mx, m, l):
        m_new = jnp.maximum(m, mx)
        alpha = jnp.exp2(m - m_new)
        p = jnp.exp2(s_ref[...] - m_new).astype(BF16)
        v1 = jnp.concatenate([v_ref[0, j * tk:(j + 1) * tk, :], ones], axis=1)
        pv = jnp.dot(p, v1, preferred_element_type=F32)
        acc_ref[...] = alpha * acc_ref[...] + pv[:, :HEAD_DIM]
        return m_new, alpha * l + pv[:, HEAD_DIM:]

    acc_ref[...] = jnp.zeros_like(acc_ref)
    m0 = jnp.full((rows, 1), NEG_BIG, F32)
    l0 = jnp.zeros((rows, HEAD_DIM), F32)
    bufs = (s_a, s_b)
    m, l = m0, l0
    mx = scores(0, bufs[0])
    for j in range(n):
        mx_next = scores(j + 1, bufs[(j + 1) % 2]) if j + 1 < n else None
        m, l = update(j, bufs[j % 2], mx, m, l)
        mx = mx_next
    o = acc_ref[...] / l
    for g in range(Q_PER_KV):
        o_ref[0, :, g * HEAD_DIM:(g + 1) * HEAD_DIM] = o[g * tq:(g + 1) * tq].astype(BF16)


def _attention(q, k, v, tq, tk):
    b, s, _ = q.shape
    grp_w = Q_PER_KV * HEAD_DIM
    rows = Q_PER_KV * tq
    return pl.pallas_call(
        functools.partial(_attn_kernel, tk=tk),
        out_shape=jax.ShapeDtypeStruct((b, s, ATTN_WIDTH), BF16),
        grid=(b, N_KV_HEADS, s // tq),
        in_specs=[
            pl.BlockSpec((1, tq, grp_w), lambda bi, h, qi: (bi, qi, h)),
            pl.BlockSpec((1, s, HEAD_DIM), lambda bi, h, qi: (bi, 0, h)),
            pl.BlockSpec((1, s, HEAD_DIM), lambda bi, h, qi: (bi, 0, h)),
        ],
        out_specs=pl.BlockSpec((1, tq, grp_w), lambda bi, h, qi: (bi, qi, h)),
        scratch_shapes=[pltpu.VMEM((rows, tk), F32), pltpu.VMEM((rows, tk), F32),
                        pltpu.VMEM((rows, HEAD_DIM), F32)],
        compiler_params=_cparams(("parallel", "parallel", "parallel")),
        name="attention",
    )(q, k, v)


def _route(logits):
    tm = logits.shape[0]
    lane = lax.broadcasted_iota(jnp.int32, (tm, ROUTE_LANES), 1)
    lane_f = lane.astype(F32)
    no_lane = float(ROUTE_LANES)
    is_g = lane < N_GROUPS
    gl = jnp.where(is_g, logits, NEG_BIG)
    gmax = jnp.max(gl, axis=1, keepdims=True)
    gsum = jnp.sum(jnp.where(is_g, jnp.exp(gl - gmax), 0.0), axis=1, keepdims=True)
    p_group = 1.0 / gsum
    gidx = jnp.min(jnp.where(is_g & (gl == gmax), lane_f, no_lane), axis=1, keepdims=True)
    lo = N_GROUPS + EXPERTS_PER_GROUP * gidx
    in_grp = (lane_f >= lo) & (lane_f < lo + EXPERTS_PER_GROUP)
    el = jnp.where(in_grp, logits, NEG_BIG)
    emax = jnp.max(el, axis=1, keepdims=True)
    j1 = jnp.min(jnp.where(in_grp & (el == emax), lane_f, no_lane), axis=1, keepdims=True)
    rest = in_grp & (lane_f != j1)
    el2 = jnp.where(rest, logits, NEG_BIG)
    e2max = jnp.max(el2, axis=1, keepdims=True)
    j2 = jnp.min(jnp.where(rest & (el2 == e2max), lane_f, no_lane), axis=1, keepdims=True)
    r = jnp.exp(e2max - emax)
    g1 = p_group / (1.0 + r)
    g2 = g1 * r
    return j1 - N_GROUPS, j2 - N_GROUPS, g1, g2


def _slot_ranks(e1, e2, counts):
    tm = e1.shape[0]
    lane_f = lax.broadcasted_iota(jnp.int32, (tm, ROUTE_LANES), 1).astype(F32)
    oh1 = lane_f == e1
    oh2 = lane_f == e2
    both = jnp.where(oh1 | oh2, 1.0, 0.0)
    earlier = (lax.broadcasted_iota(jnp.int32, (tm, tm), 1)
               < lax.broadcasted_iota(jnp.int32, (tm, tm), 0))
    before = jnp.dot(jnp.where(earlier, 1.0, 0.0).astype(BF16), both.astype(BF16),
                     preferred_element_type=F32) + counts
    rank1 = jnp.sum(jnp.where(oh1, before, 0.0), axis=1, keepdims=True)
    rank2 = jnp.sum(jnp.where(oh2, before, 0.0), axis=1, keepdims=True)
    return rank1, rank2, counts + jnp.sum(both, axis=0, keepdims=True)


def _split_bf16(a):
    hi = a.astype(BF16)
    lo = (a - hi.astype(F32)).astype(BF16)
    return hi, lo


def _mix_out_kernel(ya_ref, u_ref, up_ref, un_ref, cb_ref, mq_ref, mk_ref, mv_ref, x_ref,
                    wo_ref, gout_ref, convw_ref, convb_ref, gffn_ref, wr_hl_ref, cnt_in_ref,
                    h2_buf_ref, x1_ref, h2_ref, info_ref, route_ref, cnt_out_ref, cnt_ref, *, tiles_per_seq):
    del h2_buf_ref
    tm = x_ref.shape[0]
    i = pl.program_id(0)
    t_in_seq = i % tiles_per_seq

    @pl.when(i == 0)
    def _():
        cnt_ref[...] = cnt_in_ref[...]

    u = u_ref[...].astype(F32)
    rowi = lax.broadcasted_iota(jnp.int32, (tm, CONV_WIDTH), 0)
    prev_row = jnp.where(t_in_seq == 0, 0.0, up_ref[7:8, :].astype(F32))
    next_row = jnp.where(t_in_seq == tiles_per_seq - 1, 0.0, un_ref[0:1, :].astype(F32))
    u_prev = jnp.where(rowi == 0, prev_row, pltpu.roll(u, 1, 0))
    u_next = jnp.where(rowi == tm - 1, next_row, pltpu.roll(u, tm - 1, 0))
    conv = u_prev * convw_ref[0:1, :] + u * convw_ref[1:2, :] + u_next * convw_ref[2:3, :] + convb_ref[...]
    c_all = _rms(cb_ref[...].astype(F32) * conv, gout_ref[:, ATTN_WIDTH:ATTN_WIDTH + CONV_WIDTH])

    sub_rows = min(MIX_SUB_ROWS, tm)
    counts = cnt_ref[...]
    for r0 in range(0, tm, sub_rows):
        rs = slice(r0, r0 + sub_rows)
        a = _rms(ya_ref[rs, :].astype(F32), gout_ref[:, :ATTN_WIDTH])
        heads = []
        for h in range(N_MEM_HEADS):
            sl = slice(h * HEAD_DIM, (h + 1) * HEAD_DIM)
            s = lax.dot_general(mq_ref[rs, sl], mk_ref[0, :, sl], (((1,), (1,)), ((), ())),
                                preferred_element_type=F32)
            p = jnp.exp(s - jnp.max(s, axis=1, keepdims=True))
            o = jnp.dot(p.astype(BF16), mv_ref[0, :, sl], preferred_element_type=F32)
            heads.append(o / jnp.sum(p, axis=1, keepdims=True))
        ym = _rms(jnp.concatenate(heads, axis=1), gout_ref[:, ATTN_WIDTH + CONV_WIDTH:])

        mixed = jnp.concatenate([a, c_all[rs], ym], axis=1).astype(BF16)
        x1 = x_ref[rs, :] + jnp.dot(mixed, wo_ref[...], preferred_element_type=F32)
        x1_ref[rs, :] = x1
        h2 = _rms(x1, gffn_ref[...])
        h2_ref[rs, :] = h2
        h_hi, h_lo = _split_bf16(h2)
        hh = jnp.dot(h_hi, wr_hl_ref[...], preferred_element_type=F32)
        logits = (hh[:, :ROUTE_LANES] + hh[:, ROUTE_LANES:]
                  + jnp.dot(h_lo, wr_hl_ref[:, :ROUTE_LANES], preferred_element_type=F32))
        e1, e2, g1, g2 = _route(logits)
        rank1, rank2, counts = _slot_ranks(e1, e2, counts)
        lane = lax.broadcasted_iota(jnp.int32, (sub_rows, ROUTE_LANES), 1)
        info = jnp.zeros((sub_rows, ROUTE_LANES), F32)
        for k, col in enumerate((e1, e2, g1, g2, rank1, rank2)):
            info = jnp.where(lane == k, col, info)
        info_ref[rs, :] = info
        route_ref[:, rs] = info.T[:ROUTE_ROWS, :]
    cnt_ref[...] = counts
    cnt_out_ref[...] = counts


def _mix_out(y_attn, u, cb, mq, mk, mv, x2d, seq_len, w_o_b, g_out, conv_w, conv_b, g_ffn,
             wr_hl, tm, counts_in, h2_row0, h2_buf):
    m, d = x2d.shape
    n_mem = mk.shape[1]
    tiles_per_seq = seq_len // tm
    sub = tm // 8
    n_sub = m // 8
    h2_tile0 = h2_row0 // tm
    row = lambda i: (i, 0)
    const = lambda i: (0, 0)
    batch = lambda i: (i // tiles_per_seq, 0, 0)
    h2_buf_arg = 16
    extra_specs, extra_args = [pl.BlockSpec(memory_space=pl.ANY)], [h2_buf]
    return pl.pallas_call(
        functools.partial(_mix_out_kernel, tiles_per_seq=tiles_per_seq),
        out_shape=(jax.ShapeDtypeStruct((m, d), F32), jax.ShapeDtypeStruct(h2_buf.shape, F32),
                   jax.ShapeDtypeStruct((m, ROUTE_LANES), F32),
                   jax.ShapeDtypeStruct((ROUTE_ROWS, m), F32),
                   jax.ShapeDtypeStruct((1, ROUTE_LANES), F32)),
        grid=(m // tm,),
        input_output_aliases={h2_buf_arg: 1},
        scratch_shapes=[pltpu.VMEM((1, ROUTE_LANES), F32)],
        in_specs=[
            pl.BlockSpec((tm, ATTN_WIDTH), row),
            pl.BlockSpec((tm, CONV_WIDTH), row),
            pl.BlockSpec((8, CONV_WIDTH), lambda i: (jnp.maximum(i * sub - 1, 0), 0)),
            pl.BlockSpec((8, CONV_WIDTH), lambda i: (jnp.minimum((i + 1) * sub, n_sub - 1), 0)),
            pl.BlockSpec((tm, CONV_WIDTH), row),
            pl.BlockSpec((tm, MEM_WIDTH), row),
            pl.BlockSpec((1, n_mem, MEM_WIDTH), batch),
            pl.BlockSpec((1, n_mem, MEM_WIDTH), batch),
            pl.BlockSpec((tm, d), row),
            pl.BlockSpec((MIX_WIDTH, d), const, pipeline_mode=pl.Buffered(1)),
            pl.BlockSpec((1, MIX_WIDTH), const),
            pl.BlockSpec((3, CONV_WIDTH), const),
            pl.BlockSpec((1, CONV_WIDTH), const),
            pl.BlockSpec((1, d), const),
            pl.BlockSpec((d, 2 * ROUTE_LANES), const),
            pl.BlockSpec((1, ROUTE_LANES), const),
        ] + extra_specs,
        out_specs=(pl.BlockSpec((tm, d), row), pl.BlockSpec((tm, d), lambda i: (i + h2_tile0, 0)),
                   pl.BlockSpec((tm, ROUTE_LANES), row), pl.BlockSpec((ROUTE_ROWS, tm), lambda i: (0, i)),
                   pl.BlockSpec((1, ROUTE_LANES), const)),
        compiler_params=_cparams(("arbitrary",)),
        name="mix_out",
    )(y_attn, u, u, u, cb, mq, mk, mv, x2d, w_o_b, g_out.reshape(1, MIX_WIDTH), conv_w,
      conv_b.reshape(1, CONV_WIDTH), g_ffn.reshape(1, d), wr_hl, counts_in, *extra_args)


def _row_copy(src_hbm, idx, dst, row, sem):
    return pltpu.make_async_copy(src_hbm.at[pl.ds(idx, 1), :], dst.at[pl.ds(row, 1), :], sem)


def _experts_kernel(bexp_ref, bfirst_ref, bwslot_ref, bnext_ref, nused_ref, tok_ref,
                    h2_hbm, wg_hbm, wu_hbm, wd_hbm, ys_ref,
                    xbuf, sem, x_b, wg_f, wu_f, wd_f, wsem, wg_b, wu_b, wd_b):
    i = pl.program_id(0)
    n_used = nused_ref[0]
    blk = DISPATCH_BLOCK

    def start_gather(b, slot):
        first = bfirst_ref[b]

        def body(r, carry):
            _row_copy(h2_hbm, tok_ref[first + r], xbuf.at[slot], r, sem.at[slot]).start()
            return carry
        lax.fori_loop(0, blk, body, 0, unroll=DMA_ISSUE_UNROLL)

    def wait_gather(slot):
        pltpu.make_async_copy(h2_hbm.at[pl.ds(0, blk), :], xbuf.at[slot], sem.at[slot]).wait()

    def weight_copies(e, wslot):
        return [pltpu.make_async_copy(src.at[e], dst.at[wslot], wsem.at[wslot])
                for src, dst in ((wg_hbm, wg_f), (wu_hbm, wu_f), (wd_hbm, wd_f))]

    @pl.when(i == 0)
    def _():
        for ahead in range(GATHER_AHEAD):
            start_gather(jnp.minimum(ahead, n_used - 1), ahead)
        for c in weight_copies(bexp_ref[0], 0):
            c.start()

    @pl.when(i < n_used)
    def _():
        slot = i % N_GATHER_BUFS
        wait_gather(slot)

        @pl.when((i == 0) | (bexp_ref[i] != bexp_ref[jnp.maximum(i - 1, 0)]))
        def _():
            wslot = bwslot_ref[i]
            for c in weight_copies(bexp_ref[i], wslot):
                c.wait()
            wg_b[...] = wg_f[wslot].astype(BF16)
            wu_b[...] = wu_f[wslot].astype(BF16)
            wd_b[...] = wd_f[wslot].astype(BF16)

            @pl.when(bnext_ref[i] >= 0)
            def _():
                for c in weight_copies(bnext_ref[i], 1 - wslot):
                    c.start()

        x_b[...] = xbuf[slot].astype(BF16)
        nslot = (i + GATHER_AHEAD) % N_GATHER_BUFS
        nxt = bfirst_ref[jnp.minimum(i + GATHER_AHEAD, n_used - 1)]
        for r in range(blk):
            _row_copy(h2_hbm, tok_ref[nxt + r], xbuf.at[nslot], r,
                      sem.at[nslot]).start(priority=r % N_DMA_QUEUES)

        x = x_b[...]
        g = jnp.dot(x, wg_b[...], preferred_element_type=F32)
        up = jnp.dot(x, wu_b[...], preferred_element_type=F32)
        hid = (g / (1.0 + jnp.exp(-g))) * up
        ys_ref[...] = jnp.dot(hid.astype(BF16), wd_b[...], preferred_element_type=F32)

        @pl.when(i == n_used - 1)
        def _():
            for ahead in range(1, GATHER_AHEAD + 1):
                wait_gather((i + ahead) % N_GATHER_BUFS)

    @pl.when(i >= n_used)
    def _():
        ys_ref[...] = jnp.zeros_like(ys_ref)


def _experts(plan, sorted_tokens, h2, w_gate, w_up, w_down):
    block_expert, block_first, block_wslot, block_next, n_used = plan
    n_blocks = block_expert.shape[0]
    d = h2.shape[1]
    blk = DISPATCH_BLOCK
    hbm = pl.BlockSpec(memory_space=pl.ANY)
    grid_spec = pltpu.PrefetchScalarGridSpec(
        num_scalar_prefetch=6,
        grid=(n_blocks,),
        in_specs=[hbm, hbm, hbm, hbm],
        out_specs=pl.BlockSpec((blk, d), lambda i, *_: (i, 0)),
        scratch_shapes=[
            pltpu.VMEM((N_GATHER_BUFS, blk, d), F32),
            pltpu.SemaphoreType.DMA((N_GATHER_BUFS,)),
            pltpu.VMEM((blk, d), BF16),
            pltpu.VMEM((2, d, D_FF_EXPERT), F32),
            pltpu.VMEM((2, d, D_FF_EXPERT), F32),
            pltpu.VMEM((2, D_FF_EXPERT, d), F32),
            pltpu.SemaphoreType.DMA((2,)),
            pltpu.VMEM((d, D_FF_EXPERT), BF16),
            pltpu.VMEM((d, D_FF_EXPERT), BF16),
            pltpu.VMEM((D_FF_EXPERT, d), BF16),
        ],
    )
    return pl.pallas_call(
        _experts_kernel,
        out_shape=jax.ShapeDtypeStruct((n_blocks * blk, d), F32),
        grid_spec=grid_spec,
        compiler_params=_cparams(("arbitrary",)),
        name="experts",
    )(block_expert, block_first, block_wslot, block_next, n_used, sorted_tokens, h2, w_gate, w_up, w_down)


def _combine_kernel(pos_ref, x1_ref, info_ref, ys_hbm, out_ref, ybuf, sem, *, n, tok0, m_total):
    i = pl.program_id(0)
    tm = x1_ref.shape[0]

    def start_gather(t, slot):
        base = tok0 + t * tm
        for r in range(tm):
            for k in range(TOP_K):
                _row_copy(ys_hbm, pos_ref[base + (k * m_total + r)], ybuf.at[slot, k], r,
                          sem.at[slot]).start(priority=r % N_DMA_QUEUES)

    def start_gather_rolled(t, slot):
        def body(r, carry):
            for k in range(TOP_K):
                _row_copy(ys_hbm, pos_ref[tok0 + t * tm + r + k * m_total], ybuf.at[slot, k], r,
                          sem.at[slot]).start()
            return carry
        lax.fori_loop(0, tm, body, 0, unroll=DMA_ISSUE_UNROLL // TOP_K)

    @pl.when(i == 0)
    def _():
        for ahead in range(min(GATHER_AHEAD, n)):
            start_gather_rolled(ahead, ahead)

    @pl.when(i + GATHER_AHEAD < n)
    def _():
        start_gather(i + GATHER_AHEAD, (i + GATHER_AHEAD) % N_GATHER_BUFS)

    slot = i % N_GATHER_BUFS
    for k in range(TOP_K):
        pltpu.make_async_copy(ys_hbm.at[pl.ds(0, tm), :], ybuf.at[slot, k], sem.at[slot]).wait()
    info = info_ref[...]
    out_ref[...] = (x1_ref[...] + ybuf[slot, 0] * info[:, 2:3] + ybuf[slot, 1] * info[:, 3:4])


def _combine(pos, x1, info, ys, tm, tok0):
    m, d = x1.shape
    m_total = pos.shape[0] // TOP_K
    grid_spec = pltpu.PrefetchScalarGridSpec(
        num_scalar_prefetch=1,
        grid=(m // tm,),
        in_specs=[
            pl.BlockSpec((tm, d), lambda i, pos: (i, 0)),
            pl.BlockSpec((tm, ROUTE_LANES), lambda i, pos: (i, 0)),
            pl.BlockSpec(memory_space=pl.ANY),
        ],
        out_specs=pl.BlockSpec((tm, d), lambda i, pos: (i, 0)),
        scratch_shapes=[pltpu.VMEM((N_GATHER_BUFS, TOP_K, tm, d), F32),
                        pltpu.SemaphoreType.DMA((N_GATHER_BUFS,))],
    )
    return pl.pallas_call(
        functools.partial(_combine_kernel, n=m // tm, tok0=tok0, m_total=m_total),
        out_shape=jax.ShapeDtypeStruct((m, d), F32),
        grid_spec=grid_spec,
        compiler_params=_cparams(("arbitrary",)),
        name="combine",
    )(pos, x1, info, ys)


def _slot_pos_kernel(start_ref, e_ref, rank_ref, pos_ref):
    e = e_ref[...]
    start = jnp.zeros_like(e)
    for k in range(N_EXPERTS):
        start = jnp.where(e == k, start_ref[k], start)
    pos_ref[...] = start + rank_ref[...]


def _slot_positions(padded_start, experts2d, rank2d):
    whole = pl.BlockSpec(experts2d.shape, lambda i, start: (0, 0))
    return pl.pallas_call(
        _slot_pos_kernel,
        out_shape=jax.ShapeDtypeStruct(experts2d.shape, jnp.int32),
        grid_spec=pltpu.PrefetchScalarGridSpec(num_scalar_prefetch=1, grid=(1,), in_specs=[whole, whole],
                                               out_specs=whole),
        name="slot_positions",
    )(padded_start, experts2d, rank2d)


def _dispatch_plan(experts_flat, rank_flat, counts):
    n_slots = experts_flat.shape[0]
    blk = DISPATCH_BLOCK
    n_blocks = -(-n_slots // blk) + N_EXPERTS
    rank = rank_flat
    padded = ((counts + blk - 1) // blk) * blk
    padded_end = jnp.cumsum(padded)
    padded_start = padded_end - padded
    pos = _slot_positions(padded_start.astype(jnp.int32), experts_flat.reshape(-1, LANES),
                          rank.reshape(-1, LANES)).reshape(-1)
    sorted_tokens = jnp.concatenate([jnp.argsort(pos).astype(jnp.int32) % (n_slots // TOP_K),
                                     jnp.zeros((blk,), jnp.int32)])
    n_used = (padded_end[-1] // blk).astype(jnp.int32)
    block_start = jnp.minimum(jnp.arange(n_blocks, dtype=jnp.int32) * blk,
                              jnp.maximum(padded_end[-1] - blk, 0))
    block_expert = jnp.sum(padded_end[None, :] <= block_start[:, None], axis=1)
    block_expert = jnp.minimum(block_expert, N_EXPERTS - 1).astype(jnp.int32)
    start = jnp.cumsum(counts) - counts
    block_first = (block_start - (padded_start - start)[block_expert]).astype(jnp.int32)
    changed = jnp.concatenate([jnp.zeros((1,), jnp.int32),
                               (block_expert[1:] != block_expert[:-1]).astype(jnp.int32)])
    block_wslot = (jnp.cumsum(changed) % 2).astype(jnp.int32)
    none = N_EXPERTS
    ids = jnp.where(counts > 0, jnp.arange(N_EXPERTS, dtype=jnp.int32), none)
    next_incl = lax.cummin(ids, axis=0, reverse=True)
    next_excl = jnp.concatenate([next_incl[1:], jnp.full((1,), none, jnp.int32)])
    block_next = jnp.where(next_excl < none, next_excl, -1)[block_expert].astype(jnp.int32)
    return pos, sorted_tokens, (block_expert, block_first, block_wslot, block_next, n_used.reshape(1))


def _tile(n, pref):
    return pref if n % pref == 0 else n


def kernel(x_prompt, x_sample, mem_prompt, mem_sample, g_mix, w_in, g_q, g_k, conv_w, conv_b, g_mem, w_mem_kv, g_mq, g_mk, g_out, w_o, g_ffn, w_route_group, w_route_expert, w_gate, w_up, w_down):
    assert g_mix.shape[0] == 1, "single layer"
    d = D_MODEL
    w_in_b = w_in[0].astype(BF16)
    w_o_b = w_o[0].astype(BF16)
    w_mem_b = w_mem_kv[0].astype(BF16)
    w_route = jnp.concatenate(
        [w_route_group[0], w_route_expert[0],
         jnp.zeros((d, ROUTE_LANES - N_GROUPS - N_EXPERTS), F32)], axis=1)
    wr_hl = jnp.concatenate(_split_bf16(w_route), axis=1)

    groups = []
    m_total = x_prompt.shape[0] * x_prompt.shape[1] + x_sample.shape[0] * x_sample.shape[1]
    counts = jnp.zeros((1, ROUTE_LANES), F32)
    h2_all = jnp.zeros((m_total, d), F32)
    row0 = 0
    for x, mem in ((x_prompt, mem_prompt), (x_sample, mem_sample)):
        b, s, _ = x.shape
        tm = _tile(s, 256)
        x2d = x.reshape(b * s, d)
        cos, sin = _rope_tables(s)
        mk, mv = _mem_kv(mem, g_mem[0], w_mem_b, g_mk[0])
        q, k, v, u, cb, mq = _in_proj(x2d, s, g_mix[0], w_in_b, g_q[0], g_k[0], g_mq[0], cos, sin, tm)
        y_attn = _attention(q.reshape(b, s, ATTN_WIDTH), k.reshape(b, s, KV_WIDTH),
                            v.reshape(b, s, KV_WIDTH), _tile(s, 256), _tile(s, 512))
        x1, h2_all, info, route_t, counts = _mix_out(
            y_attn.reshape(b * s, ATTN_WIDTH), u, cb, mq, mk, mv, x2d, s, w_o_b, g_out[0], conv_w[0],
            conv_b[0], g_ffn[0], wr_hl, _tile(s, 2 * MIX_SUB_ROWS), counts, row0, h2_all)
        groups.append((x.shape, x1, info, route_t, tm, row0))
        row0 += b * s

    route_t = jnp.concatenate([g[3] for g in groups], axis=1).astype(jnp.int32)
    experts_flat = route_t[0:TOP_K].reshape(-1)
    rank_flat = route_t[4:4 + TOP_K].reshape(-1)
    pos, sorted_tokens, plan = _dispatch_plan(
        experts_flat, rank_flat, counts[0, :N_EXPERTS].astype(jnp.int32))
    ys = _experts(plan, sorted_tokens, h2_all, w_gate[0], w_up[0], w_down[0])

    outs = []
    for shape, x1, info, _, tm, tok0 in groups:
        outs.append(_combine(pos, x1, info, ys, tm, tok0).reshape(shape))
    return tuple(outs)
```

```python
import functools

import jax
import jax.numpy as jnp
from jax import lax
from jax.experimental import pallas as pl
from jax.experimental.pallas import tpu as pltpu

D_MODEL = 2048
HEAD_DIM = 128
N_Q_HEADS = 8
N_KV_HEADS = 2
Q_PER_KV = N_Q_HEADS // N_KV_HEADS
ATTN_WIDTH = N_Q_HEADS * HEAD_DIM
KV_WIDTH = N_KV_HEADS * HEAD_DIM
CONV_WIDTH = 512
N_MEM_HEADS = 4
MEM_WIDTH = N_MEM_HEADS * HEAD_DIM
MIX_WIDTH = ATTN_WIDTH + CONV_WIDTH + MEM_WIDTH
GRID_W = 64
ROPE_AXIS_DIM = HEAD_DIM // 2
ROPE_THETA = 10000.0
N_GROUPS = 8
EXPERTS_PER_GROUP = 8
N_EXPERTS = N_GROUPS * EXPERTS_PER_GROUP
TOP_K = 2
D_FF_EXPERT = 512
DISPATCH_BLOCK = 256
EPS = 1e-6

LANES = 128
ROUTE_LANES = LANES
ROUTE_ROWS = 8
V7X_VMEM_LIMIT = 56 * 1024 * 1024

F32 = jnp.float32
BF16 = jnp.bfloat16
NEG_BIG = -1e30
LOG2_E = 1.4426950408889634
DMA_ISSUE_UNROLL = 8
N_DMA_QUEUES = 2
GATHER_AHEAD = 2
N_GATHER_BUFS = GATHER_AHEAD + 1
MIX_SUB_ROWS = 256


def _rms(x, g):
    return x * lax.rsqrt(jnp.mean(x * x, axis=-1, keepdims=True) + EPS) * g


def _cparams(sem, **kw):
    return pltpu.CompilerParams(dimension_semantics=sem, vmem_limit_bytes=V7X_VMEM_LIMIT, **kw)


def _mem_kv_kernel(mem_ref, g_ref, w_ref, gk_ref, mk_ref, mv_ref):
    m = _rms(mem_ref[0], g_ref[...]).astype(BF16)
    kv = jnp.dot(m, w_ref[...], preferred_element_type=F32)
    for h in range(N_MEM_HEADS):
        sl = slice(h * HEAD_DIM, (h + 1) * HEAD_DIM)
        mk_ref[0, :, sl] = _rms(kv[:, sl], gk_ref[...]).astype(BF16)
    mv_ref[0] = kv[:, MEM_WIDTH:].astype(BF16)


def _mem_kv(mem, g_mem, w_mem_kv, g_mk):
    b, n_mem, d = mem.shape
    out = jax.ShapeDtypeStruct((b, n_mem, MEM_WIDTH), BF16)
    return pl.pallas_call(
        _mem_kv_kernel,
        out_shape=(out, out),
        grid=(b,),
        in_specs=[
            pl.BlockSpec((1, n_mem, d), lambda i: (i, 0, 0)),
            pl.BlockSpec((1, d), lambda i: (0, 0)),
            pl.BlockSpec((d, 2 * MEM_WIDTH), lambda i: (0, 0)),
            pl.BlockSpec((1, HEAD_DIM), lambda i: (0, 0)),
        ],
        out_specs=(pl.BlockSpec((1, n_mem, MEM_WIDTH), lambda i: (i, 0, 0)),
                   pl.BlockSpec((1, n_mem, MEM_WIDTH), lambda i: (i, 0, 0))),
        compiler_params=_cparams(("arbitrary",)),
        name="mem_kv",
    )(mem, g_mem.reshape(1, d), w_mem_kv, g_mk.reshape(1, HEAD_DIM))


def _rope_tables(seq_len):
    t = jnp.arange(seq_len, dtype=jnp.int32)
    row = (t // GRID_W).astype(F32)
    col = (t % GRID_W).astype(F32)
    inv_freq = ROPE_THETA ** (-jnp.arange(0, ROPE_AXIS_DIM, 2, dtype=F32) / ROPE_AXIS_DIM)
    ang_r = row[:, None] * inv_freq
    ang_c = col[:, None] * inv_freq
    cos = jnp.concatenate([jnp.cos(ang_r), jnp.cos(ang_r), jnp.cos(ang_c), jnp.cos(ang_c)], axis=-1)
    sin = jnp.concatenate([-jnp.sin(ang_r), jnp.sin(ang_r), -jnp.sin(ang_c), jnp.sin(ang_c)], axis=-1)
    return cos, sin


def _in_proj_kernel(x_ref, gmix_ref, w_ref, gq_ref, gk_ref, gmq_ref, cos_ref, sin_ref,
                    q_ref, k_ref, v_ref, u_ref, cb_ref, mq_ref):
    tm = x_ref.shape[0]
    hb = _rms(x_ref[...], gmix_ref[...]).astype(BF16)
    cos = cos_ref[...]
    sin = sin_ref[...]
    lane = lax.broadcasted_iota(jnp.int32, (tm, HEAD_DIM), 1)
    first_half = (lane % (ROPE_AXIS_DIM)) < (ROPE_AXIS_DIM // 2)
    scale = HEAD_DIM ** -0.5
    q_scale = scale * LOG2_E

    def rope(p):
        swapped = jnp.where(first_half, pltpu.roll(p, HEAD_DIM - ROPE_AXIS_DIM // 2, 1),
                            pltpu.roll(p, ROPE_AXIS_DIM // 2, 1))
        return p * cos + swapped * sin

    def proj(lo, width):
        return jnp.dot(hb, w_ref[:, lo:lo + width], preferred_element_type=F32)

    pq = proj(0, ATTN_WIDTH)
    for h in range(N_Q_HEADS):
        sl = slice(h * HEAD_DIM, (h + 1) * HEAD_DIM)
        q_ref[:, sl] = (rope(_rms(pq[:, sl], gq_ref[...])) * q_scale).astype(BF16)
    pkv = proj(ATTN_WIDTH, 2 * KV_WIDTH)
    for h in range(N_KV_HEADS):
        sl = slice(h * HEAD_DIM, (h + 1) * HEAD_DIM)
        k_ref[:, sl] = rope(_rms(pkv[:, sl], gk_ref[...])).astype(BF16)
    v_ref[...] = pkv[:, KV_WIDTH:].astype(BF16)
    pc = proj(ATTN_WIDTH + 2 * KV_WIDTH, 3 * CONV_WIDTH)
    u_ref[...] = (pc[:, 2 * CONV_WIDTH:] * pc[:, :CONV_WIDTH]).astype(BF16)
    cb_ref[...] = pc[:, CONV_WIDTH:2 * CONV_WIDTH].astype(BF16)
    pm = proj(ATTN_WIDTH + 2 * KV_WIDTH + 3 * CONV_WIDTH, MEM_WIDTH)
    for h in range(N_MEM_HEADS):
        sl = slice(h * HEAD_DIM, (h + 1) * HEAD_DIM)
        mq_ref[:, sl] = (_rms(pm[:, sl], gmq_ref[...]) * scale).astype(BF16)


def _in_proj(x2d, seq_len, g_mix, w_in_b, g_q, g_k, g_mq, cos, sin, tm):
    m, d = x2d.shape
    in_width = w_in_b.shape[1]
    tiles_per_seq = seq_len // tm
    row = lambda i: (i, 0)
    const = lambda i: (0, 0)
    pos = lambda i: (i % tiles_per_seq, 0)
    outs = [(ATTN_WIDTH,), (KV_WIDTH,), (KV_WIDTH,), (CONV_WIDTH,), (CONV_WIDTH,), (MEM_WIDTH,)]
    return pl.pallas_call(
        _in_proj_kernel,
        out_shape=tuple(jax.ShapeDtypeStruct((m, w[0]), BF16) for w in outs),
        grid=(m // tm,),
        in_specs=[
            pl.BlockSpec((tm, d), row),
            pl.BlockSpec((1, d), const),
            pl.BlockSpec((d, in_width), const, pipeline_mode=pl.Buffered(1)),
            pl.BlockSpec((1, HEAD_DIM), const),
            pl.BlockSpec((1, HEAD_DIM), const),
            pl.BlockSpec((1, HEAD_DIM), const),
            pl.BlockSpec((tm, HEAD_DIM), pos),
            pl.BlockSpec((tm, HEAD_DIM), pos),
        ],
        out_specs=tuple(pl.BlockSpec((tm, w[0]), row) for w in outs),
        compiler_params=_cparams(("parallel",)),
        name="in_proj",
    )(x2d, g_mix.reshape(1, d), w_in_b, g_q.reshape(1, HEAD_DIM), g_k.reshape(1, HEAD_DIM),
      g_mq.reshape(1, HEAD_DIM), cos, sin)


def _attn_kernel(q_ref, k_ref, v_ref, o_ref, s_a, s_b, acc_ref, *, tk):
    tq = q_ref.shape[1]
    seq = k_ref.shape[1]
    n = seq // tk
    rows = Q_PER_KV * tq
    q = jnp.concatenate([q_ref[0, :, g * HEAD_DIM:(g + 1) * HEAD_DIM] for g in range(Q_PER_KV)], axis=0)
    ones = jnp.ones((tk, HEAD_DIM), BF16)

    def scores(j, s_ref):
        s = lax.dot_general(q, k_ref[0, j * tk:(j + 1) * tk, :], (((1,), (1,)), ((), ())),
                            preferred_element_type=F32)
        s_ref[...] = s
        return jnp.max(s, axis=1, keepdims=True)

    def update(j, s_ref, mx, m, l):
        m_new = jnp.maximum(m, mx)
        alpha = jnp.exp2(m - m_new)
        p = jnp.exp2(s_ref[...] - m_new).astype(BF16)
        v1 = jnp.concatenate([v_ref[0, j * tk:(j + 1) * tk, :], ones], axis=1)
        pv = jnp.dot(p, v1, preferred_element_type=F32)
        acc_ref[...] = alpha * acc_ref[...] + pv[:, :HEAD_DIM]
        return m_new, alpha * l + pv[:, HEAD_DIM:]

    acc_ref[...] = jnp.zeros_like(acc_ref)
    m0 = jnp.full((rows, 1), NEG_BIG, F32)
    l0 = jnp.zeros((rows, HEAD_DIM), F32)
    bufs = (s_a, s_b)
    m, l = m0, l0
    mx = scores(0, bufs[0])
    for j in range(n):
        mx_next = scores(j + 1, bufs[(j + 1) % 2]) if j + 1 < n else None
        m, l = update(j, bufs[j % 2], mx, m, l)
        mx = mx_next
    o = acc_ref[...] / l
    for g in range(Q_PER_KV):
        o_ref[0, :, g * HEAD_DIM:(g + 1) * HEAD_DIM] = o[g * tq:(g + 1) * tq].astype(BF16)


def _attention(q, k, v, tq, tk):
    b, s, _ = q.shape
    grp_w = Q_PER_KV * HEAD_DIM
    rows = Q_PER_KV * tq
    return pl.pallas_call(
        functools.partial(_attn_kernel, tk=tk),
        out_shape=jax.ShapeDtypeStruct((b, s, ATTN_WIDTH), BF16),
        grid=(b, N_KV_HEADS, s // tq),
        in_specs=[
            pl.BlockSpec((1, tq, grp_w), lambda bi, h, qi: (bi, qi, h)),
            pl.BlockSpec((1, s, HEAD_DIM), lambda bi, h, qi: (bi, 0, h)),
            pl.BlockSpec((1, s, HEAD_DIM), lambda bi, h, qi: (bi, 0, h)),
        ],
        out_specs=pl.BlockSpec((1, tq, grp_w), lambda bi, h, qi: (bi, qi, h)),
        scratch_shapes=[pltpu.VMEM((rows, tk), F32), pltpu.VMEM((rows, tk), F32),
                        pltpu.VMEM((rows, HEAD_DIM), F32)],
        compiler_params=_cparams(("parallel", "parallel", "parallel")),
        name="attention",
    )(q, k, v)


def _route(logits):
    tm = logits.shape[0]
    lane = lax.broadcasted_iota(jnp.int32, (tm, ROUTE_LANES), 1)
    lane_f = lane.astype(F32)
    no_lane = float(ROUTE_LANES)
    is_g = lane < N_GROUPS
    gl = jnp.where(is_g, logits, NEG_BIG)
    gmax = jnp.max(gl, axis=1, keepdims=True)
    gsum = jnp.sum(jnp.where(is_g, jnp.exp(gl - gmax), 0.0), axis=1, keepdims=True)
    p_group = 1.0 / gsum
    gidx = jnp.min(jnp.where(is_g & (gl == gmax), lane_f, no_lane), axis=1, keepdims=True)
    lo = N_GROUPS + EXPERTS_PER_GROUP * gidx
    in_grp = (lane_f >= lo) & (lane_f < lo + EXPERTS_PER_GROUP)
    el = jnp.where(in_grp, logits, NEG_BIG)
    emax = jnp.max(el, axis=1, keepdims=True)
    j1 = jnp.min(jnp.where(in_grp & (el == emax), lane_f, no_lane), axis=1, keepdims=True)
    rest = in_grp & (lane_f != j1)
    el2 = jnp.where(rest, logits, NEG_BIG)
    e2max = jnp.max(el2, axis=1, keepdims=True)
    j2 = jnp.min(jnp.where(rest & (el2 == e2max), lane_f, no_lane), axis=1, keepdims=True)
    r = jnp.exp(e2max - emax)
    g1 = p_group / (1.0 + r)
    g2 = g1 * r
    return j1 - N_GROUPS, j2 - N_GROUPS, g1, g2


def _slot_ranks(e1, e2, counts):
    tm = e1.shape[0]
    lane_f = lax.broadcasted_iota(jnp.int32, (tm, ROUTE_LANES), 1).astype(F32)
    oh1 = lane_f == e1
    oh2 = lane_f == e2
    both = jnp.where(oh1 | oh2, 1.0, 0.0)
    earlier = (lax.broadcasted_iota(jnp.int32, (tm, tm), 1)
               < lax.broadcasted_iota(jnp.int32, (tm, tm), 0))
    before = jnp.dot(jnp.where(earlier, 1.0, 0.0).astype(BF16), both.astype(BF16),
                     preferred_element_type=F32) + counts
    rank1 = jnp.sum(jnp.where(oh1, before, 0.0), axis=1, keepdims=True)
    rank2 = jnp.sum(jnp.where(oh2, before, 0.0), axis=1, keepdims=True)
    return rank1, rank2, counts + jnp.sum(both, axis=0, keepdims=True)


def _split_bf16(a):
    hi = a.astype(BF16)
    lo = (a - hi.astype(F32)).astype(BF16)
    return hi, lo


def _mix_out_kernel(ya_ref, u_ref, up_ref, un_ref, cb_ref, mq_ref, mk_ref, mv_ref, x_ref,
                    wo_ref, gout_ref, convw_ref, convb_ref, gffn_ref, wr_hl_ref, cnt_in_ref,
                    h2_buf_ref, x1_ref, h2_ref, info_ref, route_ref, cnt_out_ref, cnt_ref, *, tiles_per_seq):
    del h2_buf_ref
    tm = x_ref.shape[0]
    i = pl.program_id(0)
    t_in_seq = i % tiles_per_seq

    @pl.when(i == 0)
    def _():
        cnt_ref[...] = cnt_in_ref[...]

    u = u_ref[...].astype(F32)
    rowi = lax.broadcasted_iota(jnp.int32, (tm, CONV_WIDTH), 0)
    prev_row = jnp.where(t_in_seq == 0, 0.0, up_ref[7:8, :].astype(F32))
    next_row = jnp.where(t_in_seq == tiles_per_seq - 1, 0.0, un_ref[0:1, :].astype(F32))
    u_prev = jnp.where(rowi == 0, prev_row, pltpu.roll(u, 1, 0))
    u_next = jnp.where(rowi == tm - 1, next_row, pltpu.roll(u, tm - 1, 0))
    conv = u_prev * convw_ref[0:1, :] + u * convw_ref[1:2, :] + u_next * convw_ref[2:3, :] + convb_ref[...]
    c_all = _rms(cb_ref[...].astype(F32) * conv, gout_ref[:, ATTN_WIDTH:ATTN_WIDTH + CONV_WIDTH])

    sub_rows = min(MIX_SUB_ROWS, tm)
    counts = cnt_ref[...]
    for r0 in range(0, tm, sub_rows):
        rs = slice(r0, r0 + sub_rows)
        a = _rms(ya_ref[rs, :].astype(F32), gout_ref[:, :ATTN_WIDTH])
        heads = []
        for h in range(N_MEM_HEADS):
            sl = slice(h * HEAD_DIM, (h + 1) * HEAD_DIM)
            s = lax.dot_general(mq_ref[rs, sl], mk_ref[0, :, sl], (((1,), (1,)), ((), ())),
                                preferred_element_type=F32)
            p = jnp.exp(s - jnp.max(s, axis=1, keepdims=True))
            o = jnp.dot(p.astype(BF16), mv_ref[0, :, sl], preferred_element_type=F32)
            heads.append(o / jnp.sum(p, axis=1, keepdims=True))
        ym = _rms(jnp.concatenate(heads, axis=1), gout_ref[:, ATTN_WIDTH + CONV_WIDTH:])

        mixed = jnp.concatenate([a, c_all[rs], ym], axis=1).astype(BF16)
        x1 = x_ref[rs, :] + jnp.dot(mixed, wo_ref[...], preferred_element_type=F32)
        x1_ref[rs, :] = x1
        h2 = _rms(x1, gffn_ref[...])
        h2_ref[rs, :] = h2
        h_hi, h_lo = _split_bf16(h2)
        hh = jnp.dot(h_hi, wr_hl_ref[...], preferred_element_type=F32)
        logits = (hh[:, :ROUTE_LANES] + hh[:, ROUTE_LANES:]
                  + jnp.dot(h_lo, wr_hl_ref[:, :ROUTE_LANES], preferred_element_type=F32))
        e1, e2, g1, g2 = _route(logits)
        rank1, rank2, counts = _slot_ranks(e1, e2, counts)
        lane = lax.broadcasted_iota(jnp.int32, (sub_rows, ROUTE_LANES), 1)
        info = jnp.zeros((sub_rows, ROUTE_LANES), F32)
        for k, col in enumerate((e1, e2, g1, g2, rank1, rank2)):
            info = jnp.where(lane == k, col, info)
        info_ref[rs, :] = info
        route_ref[:, rs] = info.T[:ROUTE_ROWS, :]
    cnt_ref[...] = counts
    cnt_out_ref[...] = counts


def _mix_out(y_attn, u, cb, mq, mk, mv, x2d, seq_len, w_o_b, g_out, conv_w, conv_b, g_ffn,
             wr_hl, tm, counts_in, h2_row0, h2_buf):
    m, d = x2d.shape
    n_mem = mk.shape[1]
    tiles_per_seq = seq_len // tm
    sub = tm // 8
    n_sub = m // 8
    h2_tile0 = h2_row0 // tm
    row = lambda i: (i, 0)
    const = lambda i: (0, 0)
    batch = lambda i: (i // tiles_per_seq, 0, 0)
    h2_buf_arg = 16
    extra_specs, extra_args = [pl.BlockSpec(memory_space=pl.ANY)], [h2_buf]
    return pl.pallas_call(
        functools.partial(_mix_out_kernel, tiles_per_seq=tiles_per_seq),
        out_shape=(jax.ShapeDtypeStruct((m, d), F32), jax.ShapeDtypeStruct(h2_buf.shape, F32),
                   jax.ShapeDtypeStruct((m, ROUTE_LANES), F32),
                   jax.ShapeDtypeStruct((ROUTE_ROWS, m), F32),
                   jax.ShapeDtypeStruct((1, ROUTE_LANES), F32)),
        grid=(m // tm,),
        input_output_aliases={h2_buf_arg: 1},
        scratch_shapes=[pltpu.VMEM((1, ROUTE_LANES), F32)],
        in_specs=[
            pl.BlockSpec((tm, ATTN_WIDTH), row),
            pl.BlockSpec((tm, CONV_WIDTH), row),
            pl.BlockSpec((8, CONV_WIDTH), lambda i: (jnp.maximum(i * sub - 1, 0), 0)),
            pl.BlockSpec((8, CONV_WIDTH), lambda i: (jnp.minimum((i + 1) * sub, n_sub - 1), 0)),
            pl.BlockSpec((tm, CONV_WIDTH), row),
            pl.BlockSpec((tm, MEM_WIDTH), row),
            pl.BlockSpec((1, n_mem, MEM_WIDTH), batch),
            pl.BlockSpec((1, n_mem, MEM_WIDTH), batch),
            pl.BlockSpec((tm, d), row),
            pl.BlockSpec((MIX_WIDTH, d), const, pipeline_mode=pl.Buffered(1)),
            pl.BlockSpec((1, MIX_WIDTH), const),
            pl.BlockSpec((3, CONV_WIDTH), const),
            pl.BlockSpec((1, CONV_WIDTH), const),
            pl.BlockSpec((1, d), const),
            pl.BlockSpec((d, 2 * ROUTE_LANES), const),
            pl.BlockSpec((1, ROUTE_LANES), const),
        ] + extra_specs,
        out_specs=(pl.BlockSpec((tm, d), row), pl.BlockSpec((tm, d), lambda i: (i + h2_tile0, 0)),
                   pl.BlockSpec((tm, ROUTE_LANES), row), pl.BlockSpec((ROUTE_ROWS, tm), lambda i: (0, i)),
                   pl.BlockSpec((1, ROUTE_LANES), const)),
        compiler_params=_cparams(("arbitrary",)),
        name="mix_out",
    )(y_attn, u, u, u, cb, mq, mk, mv, x2d, w_o_b, g_out.reshape(1, MIX_WIDTH), conv_w,
      conv_b.reshape(1, CONV_WIDTH), g_ffn.reshape(1, d), wr_hl, counts_in, *extra_args)


def _row_copy(src_hbm, idx, dst, row, sem):
    return pltpu.make_async_copy(src_hbm.at[pl.ds(idx, 1), :], dst.at[pl.ds(row, 1), :], sem)


def _experts_kernel(bexp_ref, bfirst_ref, bwslot_ref, bnext_ref, nused_ref, tok_ref,
                    h2_hbm, wg_hbm, wu_hbm, wd_hbm, ys_ref,
                    xbuf, sem, x_b, wg_f, wu_f, wd_f, wsem, wg_b, wu_b, wd_b):
    i = pl.program_id(0)
    n_used = nused_ref[0]
    blk = DISPATCH_BLOCK

    def start_gather(b, slot):
        first = bfirst_ref[b]

        def body(r, carry):
            _row_copy(h2_hbm, tok_ref[first + r], xbuf.at[slot], r, sem.at[slot]).start()
            return carry
        lax.fori_loop(0, blk, body, 0, unroll=DMA_ISSUE_UNROLL)

    def wait_gather(slot):
        pltpu.make_async_copy(h2_hbm.at[pl.ds(0, blk), :], xbuf.at[slot], sem.at[slot]).wait()

    def weight_copies(e, wslot):
        return [pltpu.make_async_copy(src.at[e], dst.at[wslot], wsem.at[wslot])
                for src, dst in ((wg_hbm, wg_f), (wu_hbm, wu_f), (wd_hbm, wd_f))]

    @pl.when(i == 0)
    def _():
        for ahead in range(GATHER_AHEAD):
            start_gather(jnp.minimum(ahead, n_used - 1), ahead)
        for c in weight_copies(bexp_ref[0], 0):
            c.start()

    @pl.when(i < n_used)
    def _():
        slot = i % N_GATHER_BUFS
        wait_gather(slot)

        @pl.when((i == 0) | (bexp_ref[i] != bexp_ref[jnp.maximum(i - 1, 0)]))
        def _():
            wslot = bwslot_ref[i]
            for c in weight_copies(bexp_ref[i], wslot):
                c.wait()
            wg_b[...] = wg_f[wslot].astype(BF16)
            wu_b[...] = wu_f[wslot].astype(BF16)
            wd_b[...] = wd_f[wslot].astype(BF16)

            @pl.when(bnext_ref[i] >= 0)
            def _():
                for c in weight_copies(bnext_ref[i], 1 - wslot):
                    c.start()

        x_b[...] = xbuf[slot].astype(BF16)
        nslot = (i + GATHER_AHEAD) % N_GATHER_BUFS
        nxt = bfirst_ref[jnp.minimum(i + GATHER_AHEAD, n_used - 1)]
        for r in range(blk):
            _row_copy(h2_hbm, tok_ref[nxt + r], xbuf.at[nslot], r,
                      sem.at[nslot]).start(priority=r % N_DMA_QUEUES)

        x = x_b[...]
        g = jnp.dot(x, wg_b[...], preferred_element_type=F32)
        up = jnp.dot(x, wu_b[...], preferred_element_type=F32)
        hid = (g / (1.0 + jnp.exp(-g))) * up
        ys_ref[...] = jnp.dot(hid.astype(BF16), wd_b[...], preferred_element_type=F32)

        @pl.when(i == n_used - 1)
        def _():
            for ahead in range(1, GATHER_AHEAD + 1):
                wait_gather((i + ahead) % N_GATHER_BUFS)

    @pl.when(i >= n_used)
    def _():
        ys_ref[...] = jnp.zeros_like(ys_ref)


def _experts(plan, sorted_tokens, h2, w_gate, w_up, w_down):
    block_expert, block_first, block_wslot, block_next, n_used = plan
    n_blocks = block_expert.shape[0]
    d = h2.shape[1]
    blk = DISPATCH_BLOCK
    hbm = pl.BlockSpec(memory_space=pl.ANY)
    grid_spec = pltpu.PrefetchScalarGridSpec(
        num_scalar_prefetch=6,
        grid=(n_blocks,),
        in_specs=[hbm, hbm, hbm, hbm],
        out_specs=pl.BlockSpec((blk, d), lambda i, *_: (i, 0)),
        scratch_shapes=[
            pltpu.VMEM((N_GATHER_BUFS, blk, d), F32),
            pltpu.SemaphoreType.DMA((N_GATHER_BUFS,)),
            pltpu.VMEM((blk, d), BF16),
            pltpu.VMEM((2, d, D_FF_EXPERT), F32),
            pltpu.VMEM((2, d, D_FF_EXPERT), F32),
            pltpu.VMEM((2, D_FF_EXPERT, d), F32),
            pltpu.SemaphoreType.DMA((2,)),
            pltpu.VMEM((d, D_FF_EXPERT), BF16),
            pltpu.VMEM((d, D_FF_EXPERT), BF16),
            pltpu.VMEM((D_FF_EXPERT, d), BF16),
        ],
    )
    return pl.pallas_call(
        _experts_kernel,
        out_shape=jax.ShapeDtypeStruct((n_blocks * blk, d), F32),
        grid_spec=grid_spec,
        compiler_params=_cparams(("arbitrary",)),
        name="experts",
    )(block_expert, block_first, block_wslot, block_next, n_used, sorted_tokens, h2, w_gate, w_up, w_down)


def _combine_kernel(pos_ref, x1_ref, info_ref, ys_hbm, out_ref, ybuf, sem, *, n, tok0, m_total):
    i = pl.program_id(0)
    tm = x1_ref.shape[0]

    def start_gather(t, slot):
        base = tok0 + t * tm
        for r in range(tm):
            for k in range(TOP_K):
                _row_copy(ys_hbm, pos_ref[base + (k * m_total + r)], ybuf.at[slot, k], r,
                          sem.at[slot]).start(priority=r % N_DMA_QUEUES)

    def start_gather_rolled(t, slot):
        def body(r, carry):
            for k in range(TOP_K):
                _row_copy(ys_hbm, pos_ref[tok0 + t * tm + r + k * m_total], ybuf.at[slot, k], r,
                          sem.at[slot]).start()
            return carry
        lax.fori_loop(0, tm, body, 0, unroll=DMA_ISSUE_UNROLL // TOP_K)

    @pl.when(i == 0)
    def _():
        for ahead in range(min(GATHER_AHEAD, n)):
            start_gather_rolled(ahead, ahead)

    @pl.when(i + GATHER_AHEAD < n)
    def _():
        start_gather(i + GATHER_AHEAD, (i + GATHER_AHEAD) % N_GATHER_BUFS)

    slot = i % N_GATHER_BUFS
    for k in range(TOP_K):
        pltpu.make_async_copy(ys_hbm.at[pl.ds(0, tm), :], ybuf.at[slot, k], sem.at[slot]).wait()
    info = info_ref[...]
    out_ref[...] = (x1_ref[...] + ybuf[slot, 0] * info[:, 2:3] + ybuf[slot, 1] * info[:, 3:4])


def _combine(pos, x1, info, ys, tm, tok0):
    m, d = x1.shape
    m_total = pos.shape[0] // TOP_K
    grid_spec = pltpu.PrefetchScalarGridSpec(
        num_scalar_prefetch=1,
        grid=(m // tm,),
        in_specs=[
            pl.BlockSpec((tm, d), lambda i, pos: (i, 0)),
            pl.BlockSpec((tm, ROUTE_LANES), lambda i, pos: (i, 0)),
            pl.BlockSpec(memory_space=pl.ANY),
        ],
        out_specs=pl.BlockSpec((tm, d), lambda i, pos: (i, 0)),
        scratch_shapes=[pltpu.VMEM((N_GATHER_BUFS, TOP_K, tm, d), F32),
                        pltpu.SemaphoreType.DMA((N_GATHER_BUFS,))],
    )
    return pl.pallas_call(
        functools.partial(_combine_kernel, n=m // tm, tok0=tok0, m_total=m_total),
        out_shape=jax.ShapeDtypeStruct((m, d), F32),
        grid_spec=grid_spec,
        compiler_params=_cparams(("arbitrary",)),
        name="combine",
    )(pos, x1, info, ys)


def _slot_pos_kernel(start_ref, e_ref, rank_ref, pos_ref):
    e = e_ref[...]
    start = jnp.zeros_like(e)
    for k in range(N_EXPERTS):
        start = jnp.where(e == k, start_ref[k], start)
    pos_ref[...] = start + rank_ref[...]


def _slot_positions(padded_start, experts2d, rank2d):
    whole = pl.BlockSpec(experts2d.shape, lambda i, start: (0, 0))
    return pl.pallas_call(
        _slot_pos_kernel,
        out_shape=jax.ShapeDtypeStruct(experts2d.shape, jnp.int32),
        grid_spec=pltpu.PrefetchScalarGridSpec(num_scalar_prefetch=1, grid=(1,), in_specs=[whole, whole],
                                               out_specs=whole),
        name="slot_positions",
    )(padded_start, experts2d, rank2d)


def _dispatch_plan(experts_flat, rank_flat, counts):
    n_slots = experts_flat.shape[0]
    blk = DISPATCH_BLOCK
    n_blocks = -(-n_slots // blk) + N_EXPERTS
    rank = rank_flat
    padded = ((counts + blk - 1) // blk) * blk
    padded_end = jnp.cumsum(padded)
    padded_start = padded_end - padded
    pos = _slot_positions(padded_start.astype(jnp.int32), experts_flat.reshape(-1, LANES),
                          rank.reshape(-1, LANES)).reshape(-1)
    sorted_tokens = jnp.concatenate([jnp.argsort(pos).astype(jnp.int32) % (n_slots // TOP_K),
                                     jnp.zeros((blk,), jnp.int32)])
    n_used = (padded_end[-1] // blk).astype(jnp.int32)
    block_start = jnp.minimum(jnp.arange(n_blocks, dtype=jnp.int32) * blk,
                              jnp.maximum(padded_end[-1] - blk, 0))
    block_expert = jnp.sum(padded_end[None, :] <= block_start[:, None], axis=1)
    block_expert = jnp.minimum(block_expert, N_EXPERTS - 1).astype(jnp.int32)
    start = jnp.cumsum(counts) - counts
    is_expert = block_expert[:, None] == jnp.arange(N_EXPERTS, dtype=jnp.int32)[None, :]

    def per_block(table):
        return jnp.sum(jnp.where(is_expert, table[None, :], 0), axis=1).astype(jnp.int32)

    block_first = block_start - per_block(padded_start - start)
    changed = jnp.concatenate([jnp.zeros((1,), jnp.int32),
                               (block_expert[1:] != block_expert[:-1]).astype(jnp.int32)])
    block_wslot = (jnp.cumsum(changed) % 2).astype(jnp.int32)
    none = N_EXPERTS
    ids = jnp.where(counts > 0, jnp.arange(N_EXPERTS, dtype=jnp.int32), none)
    next_incl = lax.cummin(ids, axis=0, reverse=True)
    next_excl = jnp.concatenate([next_incl[1:], jnp.full((1,), none, jnp.int32)])
    block_next = per_block(jnp.where(next_excl < none, next_excl, -1))
    return pos, sorted_tokens, (block_expert, block_first, block_wslot, block_next, n_used.reshape(1))


def _tile(n, pref):
    return pref if n % pref == 0 else n


def kernel(x_prompt, x_sample, mem_prompt, mem_sample, g_mix, w_in, g_q, g_k, conv_w, conv_b, g_mem, w_mem_kv, g_mq, g_mk, g_out, w_o, g_ffn, w_route_group, w_route_expert, w_gate, w_up, w_down):
    assert g_mix.shape[0] == 1, "single layer"
    d = D_MODEL
    w_in_b = w_in[0].astype(BF16)
    w_o_b = w_o[0].astype(BF16)
    w_mem_b = w_mem_kv[0].astype(BF16)
    w_route = jnp.concatenate(
        [w_route_group[0], w_route_expert[0],
         jnp.zeros((d, ROUTE_LANES - N_GROUPS - N_EXPERTS), F32)], axis=1)
    wr_hl = jnp.concatenate(_split_bf16(w_route), axis=1)

    groups = []
    m_total = x_prompt.shape[0] * x_prompt.shape[1] + x_sample.shape[0] * x_sample.shape[1]
    counts = jnp.zeros((1, ROUTE_LANES), F32)
    h2_all = jnp.zeros((m_total, d), F32)
    cos, sin = _rope_tables(max(x_prompt.shape[1], x_sample.shape[1]))
    row0 = 0
    for x, mem in ((x_prompt, mem_prompt), (x_sample, mem_sample)):
        b, s, _ = x.shape
        tm = _tile(s, 256)
        x2d = x.reshape(b * s, d)
        mk, mv = _mem_kv(mem, g_mem[0], w_mem_b, g_mk[0])
        q, k, v, u, cb, mq = _in_proj(x2d, s, g_mix[0], w_in_b, g_q[0], g_k[0], g_mq[0], cos, sin, tm)
        y_attn = _attention(q.reshape(b, s, ATTN_WIDTH), k.reshape(b, s, KV_WIDTH),
                            v.reshape(b, s, KV_WIDTH), _tile(s, 256), _tile(s, 1024))
        x1, h2_all, info, route_t, counts = _mix_out(
            y_attn.reshape(b * s, ATTN_WIDTH), u, cb, mq, mk, mv, x2d, s, w_o_b, g_out[0], conv_w[0],
            conv_b[0], g_ffn[0], wr_hl, _tile(s, 2 * MIX_SUB_ROWS), counts, row0, h2_all)
        groups.append((x.shape, x1, info, route_t, tm, row0))
        row0 += b * s

    route_t = jnp.concatenate([g[3] for g in groups], axis=1).astype(jnp.int32)
    experts_flat = route_t[0:TOP_K].reshape(-1)
    rank_flat = route_t[4:4 + TOP_K].reshape(-1)
    pos, sorted_tokens, plan = _dispatch_plan(
        experts_flat, rank_flat, counts[0, :N_EXPERTS].astype(jnp.int32))
    ys = _experts(plan, sorted_tokens, h2_all, w_gate[0], w_up[0], w_down[0])

    outs = []
    for shape, x1, info, _, tm, tok0 in groups:
        outs.append(_combine(pos, x1, info, ys, tm, tok0).reshape(shape))
    return tuple(outs)
```

```python
import functools

import jax
import jax.numpy as jnp
from jax import lax
from jax.experimental import pallas as pl
from jax.experimental.pallas import tpu as pltpu

D_MODEL = 2048
HEAD_DIM = 128
N_Q_HEADS = 8
N_KV_HEADS = 2
Q_PER_KV = N_Q_HEADS // N_KV_HEADS
ATTN_WIDTH = N_Q_HEADS * HEAD_DIM
KV_WIDTH = N_KV_HEADS * HEAD_DIM
CONV_WIDTH = 512
N_MEM_HEADS = 4
MEM_WIDTH = N_MEM_HEADS * HEAD_DIM
MIX_WIDTH = ATTN_WIDTH + CONV_WIDTH + MEM_WIDTH
GRID_W = 64
ROPE_AXIS_DIM = HEAD_DIM // 2
ROPE_THETA = 10000.0
N_GROUPS = 8
EXPERTS_PER_GROUP = 8
N_EXPERTS = N_GROUPS * EXPERTS_PER_GROUP
TOP_K = 2
D_FF_EXPERT = 512
DISPATCH_BLOCK = 256
EPS = 1e-6

LANES = 128
ROUTE_LANES = LANES
ROUTE_ROWS = 8
V7X_VMEM_LIMIT = 56 * 1024 * 1024

F32 = jnp.float32
BF16 = jnp.bfloat16
NEG_BIG = -1e30
LOG2_E = 1.4426950408889634
DMA_ISSUE_UNROLL = 8
N_DMA_QUEUES = 2
GATHER_AHEAD = 3
N_GATHER_BUFS = GATHER_AHEAD + 1
MIX_SUB_ROWS = 256


def _rms(x, g):
    return x * lax.rsqrt(jnp.mean(x * x, axis=-1, keepdims=True) + EPS) * g


def _cparams(sem, **kw):
    return pltpu.CompilerParams(dimension_semantics=sem, vmem_limit_bytes=V7X_VMEM_LIMIT, **kw)


def _mem_kv_kernel(mem_ref, g_ref, w_ref, gk_ref, mk_ref, mv_ref):
    m = _rms(mem_ref[0], g_ref[...]).astype(BF16)
    kv = jnp.dot(m, w_ref[...], preferred_element_type=F32)
    for h in range(N_MEM_HEADS):
        sl = slice(h * HEAD_DIM, (h + 1) * HEAD_DIM)
        mk_ref[0, :, sl] = _rms(kv[:, sl], gk_ref[...]).astype(BF16)
    mv_ref[0] = kv[:, MEM_WIDTH:].astype(BF16)


def _mem_kv(mem, g_mem, w_mem_kv, g_mk):
    b, n_mem, d = mem.shape
    out = jax.ShapeDtypeStruct((b, n_mem, MEM_WIDTH), BF16)
    return pl.pallas_call(
        _mem_kv_kernel,
        out_shape=(out, out),
        grid=(b,),
        in_specs=[
            pl.BlockSpec((1, n_mem, d), lambda i: (i, 0, 0)),
            pl.BlockSpec((1, d), lambda i: (0, 0)),
            pl.BlockSpec((d, 2 * MEM_WIDTH), lambda i: (0, 0)),
            pl.BlockSpec((1, HEAD_DIM), lambda i: (0, 0)),
        ],
        out_specs=(pl.BlockSpec((1, n_mem, MEM_WIDTH), lambda i: (i, 0, 0)),
                   pl.BlockSpec((1, n_mem, MEM_WIDTH), lambda i: (i, 0, 0))),
        compiler_params=_cparams(("arbitrary",)),
        name="mem_kv",
    )(mem, g_mem.reshape(1, d), w_mem_kv, g_mk.reshape(1, HEAD_DIM))


def _rope_tables(seq_len):
    t = jnp.arange(seq_len, dtype=jnp.int32)
    row = (t // GRID_W).astype(F32)
    col = (t % GRID_W).astype(F32)
    inv_freq = ROPE_THETA ** (-jnp.arange(0, ROPE_AXIS_DIM, 2, dtype=F32) / ROPE_AXIS_DIM)
    ang_r = row[:, None] * inv_freq
    ang_c = col[:, None] * inv_freq
    cos = jnp.concatenate([jnp.cos(ang_r), jnp.cos(ang_r), jnp.cos(ang_c), jnp.cos(ang_c)], axis=-1)
    sin = jnp.concatenate([-jnp.sin(ang_r), jnp.sin(ang_r), -jnp.sin(ang_c), jnp.sin(ang_c)], axis=-1)
    return cos, sin


def _in_proj_kernel(x_ref, gmix_ref, w_ref, gq_ref, gk_ref, gmq_ref, cos_ref, sin_ref,
                    q_ref, k_ref, v_ref, u_ref, cb_ref, mq_ref):
    tm = x_ref.shape[0]
    hb = _rms(x_ref[...], gmix_ref[...]).astype(BF16)
    cos = cos_ref[...]
    sin = sin_ref[...]
    lane = lax.broadcasted_iota(jnp.int32, (tm, HEAD_DIM), 1)
    first_half = (lane % (ROPE_AXIS_DIM)) < (ROPE_AXIS_DIM // 2)
    scale = HEAD_DIM ** -0.5
    q_scale = scale * LOG2_E

    def rope(p):
        swapped = jnp.where(first_half, pltpu.roll(p, HEAD_DIM - ROPE_AXIS_DIM // 2, 1),
                            pltpu.roll(p, ROPE_AXIS_DIM // 2, 1))
        return p * cos + swapped * sin

    def proj(lo, width):
        return jnp.dot(hb, w_ref[:, lo:lo + width], preferred_element_type=F32)

    pq = proj(0, ATTN_WIDTH)
    for h in range(N_Q_HEADS):
        sl = slice(h * HEAD_DIM, (h + 1) * HEAD_DIM)
        q_ref[:, sl] = (rope(_rms(pq[:, sl], gq_ref[...])) * q_scale).astype(BF16)
    pkv = proj(ATTN_WIDTH, 2 * KV_WIDTH)
    for h in range(N_KV_HEADS):
        sl = slice(h * HEAD_DIM, (h + 1) * HEAD_DIM)
        k_ref[:, sl] = rope(_rms(pkv[:, sl], gk_ref[...])).astype(BF16)
    v_ref[...] = pkv[:, KV_WIDTH:].astype(BF16)
    pc = proj(ATTN_WIDTH + 2 * KV_WIDTH, 3 * CONV_WIDTH)
    u_ref[...] = (pc[:, 2 * CONV_WIDTH:] * pc[:, :CONV_WIDTH]).astype(BF16)
    cb_ref[...] = pc[:, CONV_WIDTH:2 * CONV_WIDTH].astype(BF16)
    pm = proj(ATTN_WIDTH + 2 * KV_WIDTH + 3 * CONV_WIDTH, MEM_WIDTH)
    for h in range(N_MEM_HEADS):
        sl = slice(h * HEAD_DIM, (h + 1) * HEAD_DIM)
        mq_ref[:, sl] = (_rms(pm[:, sl], gmq_ref[...]) * scale).astype(BF16)


def _in_proj(x2d, seq_len, g_mix, w_in_b, g_q, g_k, g_mq, cos, sin, tm):
    m, d = x2d.shape
    in_width = w_in_b.shape[1]
    tiles_per_seq = seq_len // tm
    row = lambda i: (i, 0)
    const = lambda i: (0, 0)
    pos = lambda i: (i % tiles_per_seq, 0)
    outs = [(ATTN_WIDTH,), (KV_WIDTH,), (KV_WIDTH,), (CONV_WIDTH,), (CONV_WIDTH,), (MEM_WIDTH,)]
    return pl.pallas_call(
        _in_proj_kernel,
        out_shape=tuple(jax.ShapeDtypeStruct((m, w[0]), BF16) for w in outs),
        grid=(m // tm,),
        in_specs=[
            pl.BlockSpec((tm, d), row),
            pl.BlockSpec((1, d), const),
            pl.BlockSpec((d, in_width), const, pipeline_mode=pl.Buffered(1)),
            pl.BlockSpec((1, HEAD_DIM), const),
            pl.BlockSpec((1, HEAD_DIM), const),
            pl.BlockSpec((1, HEAD_DIM), const),
            pl.BlockSpec((tm, HEAD_DIM), pos),
            pl.BlockSpec((tm, HEAD_DIM), pos),
        ],
        out_specs=tuple(pl.BlockSpec((tm, w[0]), row) for w in outs),
        compiler_params=_cparams(("parallel",)),
        name="in_proj",
    )(x2d, g_mix.reshape(1, d), w_in_b, g_q.reshape(1, HEAD_DIM), g_k.reshape(1, HEAD_DIM),
      g_mq.reshape(1, HEAD_DIM), cos, sin)


def _attn_kernel(q_ref, k_ref, v_ref, o_ref, s_a, s_b, acc_ref, *, tk):
    tq = q_ref.shape[1]
    seq = k_ref.shape[1]
    n = seq // tk
    rows = Q_PER_KV * tq
    q = jnp.concatenate([q_ref[0, :, g * HEAD_DIM:(g + 1) * HEAD_DIM] for g in range(Q_PER_KV)], axis=0)
    ones = jnp.ones((tk, HEAD_DIM), BF16)

    def scores(j, s_ref):
        s = lax.dot_general(q, k_ref[0, j * tk:(j + 1) * tk, :], (((1,), (1,)), ((), ())),
                            preferred_element_type=F32)
        s_ref[...] = s
        return jnp.max(s, axis=1, keepdims=True)

    def update(j, s_ref, mx, m, l):
        m_new = jnp.maximum(m, mx)
        alpha = jnp.exp2(m - m_new)
        p = jnp.exp2(s_ref[...] - m_new).astype(BF16)
        v1 = jnp.concatenate([v_ref[0, j * tk:(j + 1) * tk, :], ones], axis=1)
        pv = jnp.dot(p, v1, preferred_element_type=F32)
        acc_ref[...] = alpha * acc_ref[...] + pv[:, :HEAD_DIM]
        return m_new, alpha * l + pv[:, HEAD_DIM:]

    acc_ref[...] = jnp.zeros_like(acc_ref)
    m0 = jnp.full((rows, 1), NEG_BIG, F32)
    l0 = jnp.zeros((rows, HEAD_DIM), F32)
    bufs = (s_a, s_b)
    m, l = m0, l0
    mx = scores(0, bufs[0])
    for j in range(n):
        mx_next = scores(j + 1, bufs[(j + 1) % 2]) if j + 1 < n else None
        m, l = update(j, bufs[j % 2], mx, m, l)
        mx = mx_next
    o = acc_ref[...] / l
    for g in range(Q_PER_KV):
        o_ref[0, :, g * HEAD_DIM:(g + 1) * HEAD_DIM] = o[g * tq:(g + 1) * tq].astype(BF16)


def _attention(q, k, v, tq, tk):
    b, s, _ = q.shape
    grp_w = Q_PER_KV * HEAD_DIM
    rows = Q_PER_KV * tq
    return pl.pallas_call(
        functools.partial(_attn_kernel, tk=tk),
        out_shape=jax.ShapeDtypeStruct((b, s, ATTN_WIDTH), BF16),
        grid=(b, N_KV_HEADS, s // tq),
        in_specs=[
            pl.BlockSpec((1, tq, grp_w), lambda bi, h, qi: (bi, qi, h)),
            pl.BlockSpec((1, s, HEAD_DIM), lambda bi, h, qi: (bi, 0, h)),
            pl.BlockSpec((1, s, HEAD_DIM), lambda bi, h, qi: (bi, 0, h)),
        ],
        out_specs=pl.BlockSpec((1, tq, grp_w), lambda bi, h, qi: (bi, qi, h)),
        scratch_shapes=[pltpu.VMEM((rows, tk), F32), pltpu.VMEM((rows, tk), F32),
                        pltpu.VMEM((rows, HEAD_DIM), F32)],
        compiler_params=_cparams(("parallel", "parallel", "parallel")),
        name="attention",
    )(q, k, v)


def _route(logits):
    tm = logits.shape[0]
    lane = lax.broadcasted_iota(jnp.int32, (tm, ROUTE_LANES), 1)
    lane_f = lane.astype(F32)
    no_lane = float(ROUTE_LANES)
    is_g = lane < N_GROUPS
    gl = jnp.where(is_g, logits, NEG_BIG)
    gmax = jnp.max(gl, axis=1, keepdims=True)
    gsum = jnp.sum(jnp.where(is_g, jnp.exp(gl - gmax), 0.0), axis=1, keepdims=True)
    p_group = 1.0 / gsum
    gidx = jnp.min(jnp.where(is_g & (gl == gmax), lane_f, no_lane), axis=1, keepdims=True)
    lo = N_GROUPS + EXPERTS_PER_GROUP * gidx
    in_grp = (lane_f >= lo) & (lane_f < lo + EXPERTS_PER_GROUP)
    el = jnp.where(in_grp, logits, NEG_BIG)
    emax = jnp.max(el, axis=1, keepdims=True)
    j1 = jnp.min(jnp.where(in_grp & (el == emax), lane_f, no_lane), axis=1, keepdims=True)
    rest = in_grp & (lane_f != j1)
    el2 = jnp.where(rest, logits, NEG_BIG)
    e2max = jnp.max(el2, axis=1, keepdims=True)
    j2 = jnp.min(jnp.where(rest & (el2 == e2max), lane_f, no_lane), axis=1, keepdims=True)
    r = jnp.exp(e2max - emax)
    g1 = p_group / (1.0 + r)
    g2 = g1 * r
    return j1 - N_GROUPS, j2 - N_GROUPS, g1, g2


def _slot_ranks(e1, e2, counts):
    tm = e1.shape[0]
    lane_f = lax.broadcasted_iota(jnp.int32, (tm, ROUTE_LANES), 1).astype(F32)
    oh1 = lane_f == e1
    oh2 = lane_f == e2
    both = jnp.where(oh1 | oh2, 1.0, 0.0)
    earlier = (lax.broadcasted_iota(jnp.int32, (tm, tm), 1)
               < lax.broadcasted_iota(jnp.int32, (tm, tm), 0))
    before = jnp.dot(jnp.where(earlier, 1.0, 0.0).astype(BF16), both.astype(BF16),
                     preferred_element_type=F32) + counts
    rank1 = jnp.sum(jnp.where(oh1, before, 0.0), axis=1, keepdims=True)
    rank2 = jnp.sum(jnp.where(oh2, before, 0.0), axis=1, keepdims=True)
    return rank1, rank2, counts + jnp.sum(both, axis=0, keepdims=True)


def _split_bf16(a):
    hi = a.astype(BF16)
    lo = (a - hi.astype(F32)).astype(BF16)
    return hi, lo


def _mix_out_kernel(ya_ref, u_ref, up_ref, un_ref, cb_ref, mq_ref, mk_ref, mv_ref, x_ref,
                    wo_ref, gout_ref, convw_ref, convb_ref, gffn_ref, wr_hl_ref, cnt_in_ref,
                    h2_buf_ref, x1_ref, h2_ref, info_ref, route_ref, cnt_out_ref, cnt_ref, *, tiles_per_seq):
    del h2_buf_ref
    tm = x_ref.shape[0]
    i = pl.program_id(0)
    t_in_seq = i % tiles_per_seq

    @pl.when(i == 0)
    def _():
        cnt_ref[...] = cnt_in_ref[...]

    u = u_ref[...].astype(F32)
    rowi = lax.broadcasted_iota(jnp.int32, (tm, CONV_WIDTH), 0)
    prev_row = jnp.where(t_in_seq == 0, 0.0, up_ref[7:8, :].astype(F32))
    next_row = jnp.where(t_in_seq == tiles_per_seq - 1, 0.0, un_ref[0:1, :].astype(F32))
    u_prev = jnp.where(rowi == 0, prev_row, pltpu.roll(u, 1, 0))
    u_next = jnp.where(rowi == tm - 1, next_row, pltpu.roll(u, tm - 1, 0))
    conv = u_prev * convw_ref[0:1, :] + u * convw_ref[1:2, :] + u_next * convw_ref[2:3, :] + convb_ref[...]
    c_all = _rms(cb_ref[...].astype(F32) * conv, gout_ref[:, ATTN_WIDTH:ATTN_WIDTH + CONV_WIDTH])

    sub_rows = min(MIX_SUB_ROWS, tm)
    counts = cnt_ref[...]
    for r0 in range(0, tm, sub_rows):
        rs = slice(r0, r0 + sub_rows)
        a = _rms(ya_ref[rs, :].astype(F32), gout_ref[:, :ATTN_WIDTH])
        heads = []
        for h in range(N_MEM_HEADS):
            sl = slice(h * HEAD_DIM, (h + 1) * HEAD_DIM)
            s = lax.dot_general(mq_ref[rs, sl], mk_ref[0, :, sl], (((1,), (1,)), ((), ())),
                                preferred_element_type=F32)
            p = jnp.exp(s - jnp.max(s, axis=1, keepdims=True))
            o = jnp.dot(p.astype(BF16), mv_ref[0, :, sl], preferred_element_type=F32)
            heads.append(o / jnp.sum(p, axis=1, keepdims=True))
        ym = _rms(jnp.concatenate(heads, axis=1), gout_ref[:, ATTN_WIDTH + CONV_WIDTH:])

        mixed = jnp.concatenate([a, c_all[rs], ym], axis=1).astype(BF16)
        x1 = x_ref[rs, :] + jnp.dot(mixed, wo_ref[...], preferred_element_type=F32)
        x1_ref[rs, :] = x1
        h2 = _rms(x1, gffn_ref[...])
        h2_ref[rs, :] = h2
        h_hi, h_lo = _split_bf16(h2)
        hh = jnp.dot(h_hi, wr_hl_ref[...], preferred_element_type=F32)
        logits = (hh[:, :ROUTE_LANES] + hh[:, ROUTE_LANES:]
                  + jnp.dot(h_lo, wr_hl_ref[:, :ROUTE_LANES], preferred_element_type=F32))
        e1, e2, g1, g2 = _route(logits)
        rank1, rank2, counts = _slot_ranks(e1, e2, counts)
        lane = lax.broadcasted_iota(jnp.int32, (sub_rows, ROUTE_LANES), 1)
        info = jnp.zeros((sub_rows, ROUTE_LANES), F32)
        for k, col in enumerate((e1, e2, g1, g2, rank1, rank2)):
            info = jnp.where(lane == k, col, info)
        info_ref[rs, :] = info
        route_ref[:, rs] = info.T[:ROUTE_ROWS, :]
    cnt_ref[...] = counts
    cnt_out_ref[...] = counts


def _mix_out(y_attn, u, cb, mq, mk, mv, x2d, seq_len, w_o_b, g_out, conv_w, conv_b, g_ffn,
             wr_hl, tm, counts_in, h2_row0, h2_buf):
    m, d = x2d.shape
    n_mem = mk.shape[1]
    tiles_per_seq = seq_len // tm
    sub = tm // 8
    n_sub = m // 8
    h2_tile0 = h2_row0 // tm
    row = lambda i: (i, 0)
    const = lambda i: (0, 0)
    batch = lambda i: (i // tiles_per_seq, 0, 0)
    h2_buf_arg = 16
    extra_specs, extra_args = [pl.BlockSpec(memory_space=pl.ANY)], [h2_buf]
    return pl.pallas_call(
        functools.partial(_mix_out_kernel, tiles_per_seq=tiles_per_seq),
        out_shape=(jax.ShapeDtypeStruct((m, d), F32), jax.ShapeDtypeStruct(h2_buf.shape, F32),
                   jax.ShapeDtypeStruct((m, ROUTE_LANES), F32),
                   jax.ShapeDtypeStruct((ROUTE_ROWS, m), F32),
                   jax.ShapeDtypeStruct((1, ROUTE_LANES), F32)),
        grid=(m // tm,),
        input_output_aliases={h2_buf_arg: 1},
        scratch_shapes=[pltpu.VMEM((1, ROUTE_LANES), F32)],
        in_specs=[
            pl.BlockSpec((tm, ATTN_WIDTH), row),
            pl.BlockSpec((tm, CONV_WIDTH), row),
            pl.BlockSpec((8, CONV_WIDTH), lambda i: (jnp.maximum(i * sub - 1, 0), 0)),
            pl.BlockSpec((8, CONV_WIDTH), lambda i: (jnp.minimum((i + 1) * sub, n_sub - 1), 0)),
            pl.BlockSpec((tm, CONV_WIDTH), row),
            pl.BlockSpec((tm, MEM_WIDTH), row),
            pl.BlockSpec((1, n_mem, MEM_WIDTH), batch),
            pl.BlockSpec((1, n_mem, MEM_WIDTH), batch),
            pl.BlockSpec((tm, d), row),
            pl.BlockSpec((MIX_WIDTH, d), const, pipeline_mode=pl.Buffered(1)),
            pl.BlockSpec((1, MIX_WIDTH), const),
            pl.BlockSpec((3, CONV_WIDTH), const),
            pl.BlockSpec((1, CONV_WIDTH), const),
            pl.BlockSpec((1, d), const),
            pl.BlockSpec((d, 2 * ROUTE_LANES), const),
            pl.BlockSpec((1, ROUTE_LANES), const),
        ] + extra_specs,
        out_specs=(pl.BlockSpec((tm, d), row), pl.BlockSpec((tm, d), lambda i: (i + h2_tile0, 0)),
                   pl.BlockSpec((tm, ROUTE_LANES), row), pl.BlockSpec((ROUTE_ROWS, tm), lambda i: (0, i)),
                   pl.BlockSpec((1, ROUTE_LANES), const)),
        compiler_params=_cparams(("arbitrary",)),
        name="mix_out",
    )(y_attn, u, u, u, cb, mq, mk, mv, x2d, w_o_b, g_out.reshape(1, MIX_WIDTH), conv_w,
      conv_b.reshape(1, CONV_WIDTH), g_ffn.reshape(1, d), wr_hl, counts_in, *extra_args)


def _row_copy(src_hbm, idx, dst, row, sem):
    return pltpu.make_async_copy(src_hbm.at[pl.ds(idx, 1), :], dst.at[pl.ds(row, 1), :], sem)


def _experts_kernel(bexp_ref, bfirst_ref, bwslot_ref, bnext_ref, nused_ref, tok_ref,
                    h2_hbm, wg_hbm, wu_hbm, wd_hbm, ys_ref,
                    xbuf, sem, x_b, wg_f, wu_f, wd_f, wsem, wg_b, wu_b, wd_b):
    i = pl.program_id(0)
    n_used = nused_ref[0]
    blk = DISPATCH_BLOCK

    def start_gather(b, slot):
        first = bfirst_ref[b]

        def body(r, carry):
            _row_copy(h2_hbm, tok_ref[first + r], xbuf.at[slot], r, sem.at[slot]).start()
            return carry
        lax.fori_loop(0, blk, body, 0, unroll=DMA_ISSUE_UNROLL)

    def wait_gather(slot):
        pltpu.make_async_copy(h2_hbm.at[pl.ds(0, blk), :], xbuf.at[slot], sem.at[slot]).wait()

    def weight_copies(e, wslot):
        return [pltpu.make_async_copy(src.at[e], dst.at[wslot], wsem.at[wslot])
                for src, dst in ((wg_hbm, wg_f), (wu_hbm, wu_f), (wd_hbm, wd_f))]

    @pl.when(i == 0)
    def _():
        for ahead in range(GATHER_AHEAD):
            start_gather(jnp.minimum(ahead, n_used - 1), ahead)
        for c in weight_copies(bexp_ref[0], 0):
            c.start()

    @pl.when(i < n_used)
    def _():
        slot = i % N_GATHER_BUFS
        wait_gather(slot)

        @pl.when((i == 0) | (bexp_ref[i] != bexp_ref[jnp.maximum(i - 1, 0)]))
        def _():
            wslot = bwslot_ref[i]
            for c in weight_copies(bexp_ref[i], wslot):
                c.wait()
            wg_b[...] = wg_f[wslot].astype(BF16)
            wu_b[...] = wu_f[wslot].astype(BF16)
            wd_b[...] = wd_f[wslot].astype(BF16)

            @pl.when(bnext_ref[i] >= 0)
            def _():
                for c in weight_copies(bnext_ref[i], 1 - wslot):
                    c.start()

        x_b[...] = xbuf[slot].astype(BF16)
        nslot = (i + GATHER_AHEAD) % N_GATHER_BUFS
        nxt = bfirst_ref[jnp.minimum(i + GATHER_AHEAD, n_used - 1)]
        for r in range(blk):
            _row_copy(h2_hbm, tok_ref[nxt + r], xbuf.at[nslot], r,
                      sem.at[nslot]).start(priority=r % N_DMA_QUEUES)

        x = x_b[...]
        g = jnp.dot(x, wg_b[...], preferred_element_type=F32)
        up = jnp.dot(x, wu_b[...], preferred_element_type=F32)
        hid = (g / (1.0 + jnp.exp(-g))) * up
        ys_ref[...] = jnp.dot(hid.astype(BF16), wd_b[...], preferred_element_type=F32)

        @pl.when(i == n_used - 1)
        def _():
            for ahead in range(1, GATHER_AHEAD + 1):
                wait_gather((i + ahead) % N_GATHER_BUFS)

    @pl.when(i >= n_used)
    def _():
        ys_ref[...] = jnp.zeros_like(ys_ref)


def _experts(plan, sorted_tokens, h2, w_gate, w_up, w_down):
    block_expert, block_first, block_wslot, block_next, n_used = plan
    n_blocks = block_expert.shape[0]
    d = h2.shape[1]
    blk = DISPATCH_BLOCK
    hbm = pl.BlockSpec(memory_space=pl.ANY)
    grid_spec = pltpu.PrefetchScalarGridSpec(
        num_scalar_prefetch=6,
        grid=(n_blocks,),
        in_specs=[hbm, hbm, hbm, hbm],
        out_specs=pl.BlockSpec((blk, d), lambda i, *_: (i, 0)),
        scratch_shapes=[
            pltpu.VMEM((N_GATHER_BUFS, blk, d), F32),
            pltpu.SemaphoreType.DMA((N_GATHER_BUFS,)),
            pltpu.VMEM((blk, d), BF16),
            pltpu.VMEM((2, d, D_FF_EXPERT), F32),
            pltpu.VMEM((2, d, D_FF_EXPERT), F32),
            pltpu.VMEM((2, D_FF_EXPERT, d), F32),
            pltpu.SemaphoreType.DMA((2,)),
            pltpu.VMEM((d, D_FF_EXPERT), BF16),
            pltpu.VMEM((d, D_FF_EXPERT), BF16),
            pltpu.VMEM((D_FF_EXPERT, d), BF16),
        ],
    )
    return pl.pallas_call(
        _experts_kernel,
        out_shape=jax.ShapeDtypeStruct((n_blocks * blk, d), F32),
        grid_spec=grid_spec,
        compiler_params=_cparams(("arbitrary",)),
        name="experts",
    )(block_expert, block_first, block_wslot, block_next, n_used, sorted_tokens, h2, w_gate, w_up, w_down)


def _combine_kernel(pos_ref, x1_ref, info_ref, ys_hbm, out_ref, ybuf, sem, *, n, tok0, m_total):
    i = pl.program_id(0)
    tm = x1_ref.shape[0]

    def start_gather(t, slot):
        base = tok0 + t * tm
        for r in range(tm):
            for k in range(TOP_K):
                _row_copy(ys_hbm, pos_ref[base + (k * m_total + r)], ybuf.at[slot, k], r,
                          sem.at[slot]).start(priority=r % N_DMA_QUEUES)

    def start_gather_rolled(t, slot):
        def body(r, carry):
            for k in range(TOP_K):
                _row_copy(ys_hbm, pos_ref[tok0 + t * tm + r + k * m_total], ybuf.at[slot, k], r,
                          sem.at[slot]).start()
            return carry
        lax.fori_loop(0, tm, body, 0, unroll=DMA_ISSUE_UNROLL // TOP_K)

    @pl.when(i == 0)
    def _():
        for ahead in range(min(GATHER_AHEAD, n)):
            start_gather_rolled(ahead, ahead)

    @pl.when(i + GATHER_AHEAD < n)
    def _():
        start_gather(i + GATHER_AHEAD, (i + GATHER_AHEAD) % N_GATHER_BUFS)

    slot = i % N_GATHER_BUFS
    for k in range(TOP_K):
        pltpu.make_async_copy(ys_hbm.at[pl.ds(0, tm), :], ybuf.at[slot, k], sem.at[slot]).wait()
    info = info_ref[...]
    out_ref[...] = (x1_ref[...] + ybuf[slot, 0] * info[:, 2:3] + ybuf[slot, 1] * info[:, 3:4])


def _combine(pos, x1, info, ys, tm, tok0):
    m, d = x1.shape
    m_total = pos.shape[0] // TOP_K
    grid_spec = pltpu.PrefetchScalarGridSpec(
        num_scalar_prefetch=1,
        grid=(m // tm,),
        in_specs=[
            pl.BlockSpec((tm, d), lambda i, pos: (i, 0)),
            pl.BlockSpec((tm, ROUTE_LANES), lambda i, pos: (i, 0)),
            pl.BlockSpec(memory_space=pl.ANY),
        ],
        out_specs=pl.BlockSpec((tm, d), lambda i, pos: (i, 0)),
        scratch_shapes=[pltpu.VMEM((N_GATHER_BUFS, TOP_K, tm, d), F32),
                        pltpu.SemaphoreType.DMA((N_GATHER_BUFS,))],
    )
    return pl.pallas_call(
        functools.partial(_combine_kernel, n=m // tm, tok0=tok0, m_total=m_total),
        out_shape=jax.ShapeDtypeStruct((m, d), F32),
        grid_spec=grid_spec,
        compiler_params=_cparams(("arbitrary",)),
        name="combine",
    )(pos, x1, info, ys)


def _slot_pos_kernel(start_ref, e_ref, rank_ref, pos_ref):
    e = e_ref[...]
    start = jnp.zeros_like(e)
    for k in range(N_EXPERTS):
        start = jnp.where(e == k, start_ref[k], start)
    pos_ref[...] = start + rank_ref[...]


def _slot_positions(padded_start, experts2d, rank2d):
    whole = pl.BlockSpec(experts2d.shape, lambda i, start: (0, 0))
    return pl.pallas_call(
        _slot_pos_kernel,
        out_shape=jax.ShapeDtypeStruct(experts2d.shape, jnp.int32),
        grid_spec=pltpu.PrefetchScalarGridSpec(num_scalar_prefetch=1, grid=(1,), in_specs=[whole, whole],
                                               out_specs=whole),
        name="slot_positions",
    )(padded_start, experts2d, rank2d)


def _dispatch_plan(experts_flat, rank_flat, counts):
    n_slots = experts_flat.shape[0]
    blk = DISPATCH_BLOCK
    n_blocks = -(-n_slots // blk) + N_EXPERTS
    rank = rank_flat
    padded = ((counts + blk - 1) // blk) * blk
    padded_end = jnp.cumsum(padded)
    padded_start = padded_end - padded
    pos = _slot_positions(padded_start.astype(jnp.int32), experts_flat.reshape(-1, LANES),
                          rank.reshape(-1, LANES)).reshape(-1)
    sorted_tokens = jnp.concatenate([jnp.argsort(pos).astype(jnp.int32) % (n_slots // TOP_K),
                                     jnp.zeros((blk,), jnp.int32)])
    n_used = (padded_end[-1] // blk).astype(jnp.int32)
    block_start = jnp.minimum(jnp.arange(n_blocks, dtype=jnp.int32) * blk,
                              jnp.maximum(padded_end[-1] - blk, 0))
    block_expert = jnp.sum(padded_end[None, :] <= block_start[:, None], axis=1)
    block_expert = jnp.minimum(block_expert, N_EXPERTS - 1).astype(jnp.int32)
    start = jnp.cumsum(counts) - counts
    is_expert = block_expert[:, None] == jnp.arange(N_EXPERTS, dtype=jnp.int32)[None, :]

    def per_block(table):
        return jnp.sum(jnp.where(is_expert, table[None, :], 0), axis=1).astype(jnp.int32)

    block_first = block_start - per_block(padded_start - start)
    changed = jnp.concatenate([jnp.zeros((1,), jnp.int32),
                               (block_expert[1:] != block_expert[:-1]).astype(jnp.int32)])
    block_wslot = (jnp.cumsum(changed) % 2).astype(jnp.int32)
    none = N_EXPERTS
    ids = jnp.where(counts > 0, jnp.arange(N_EXPERTS, dtype=jnp.int32), none)
    next_incl = lax.cummin(ids, axis=0, reverse=True)
    next_excl = jnp.concatenate([next_incl[1:], jnp.full((1,), none, jnp.int32)])
    block_next = per_block(jnp.where(next_excl < none, next_excl, -1))
    return pos, sorted_tokens, (block_expert, block_first, block_wslot, block_next, n_used.reshape(1))


def _tile(n, pref):
    return pref if n % pref == 0 else n


def kernel(x_prompt, x_sample, mem_prompt, mem_sample, g_mix, w_in, g_q, g_k, conv_w, conv_b, g_mem, w_mem_kv, g_mq, g_mk, g_out, w_o, g_ffn, w_route_group, w_route_expert, w_gate, w_up, w_down):
    assert g_mix.shape[0] == 1, "single layer"
    d = D_MODEL
    w_in_b = w_in[0].astype(BF16)
    w_o_b = w_o[0].astype(BF16)
    w_mem_b = w_mem_kv[0].astype(BF16)
    w_route = jnp.concatenate(
        [w_route_group[0], w_route_expert[0],
         jnp.zeros((d, ROUTE_LANES - N_GROUPS - N_EXPERTS), F32)], axis=1)
    wr_hl = jnp.concatenate(_split_bf16(w_route), axis=1)

    groups = []
    m_total = x_prompt.shape[0] * x_prompt.shape[1] + x_sample.shape[0] * x_sample.shape[1]
    counts = jnp.zeros((1, ROUTE_LANES), F32)
    h2_all = jnp.zeros((m_total, d), F32)
    cos, sin = _rope_tables(max(x_prompt.shape[1], x_sample.shape[1]))
    row0 = 0
    for x, mem in ((x_prompt, mem_prompt), (x_sample, mem_sample)):
        b, s, _ = x.shape
        tm = _tile(s, 256)
        x2d = x.reshape(b * s, d)
        mk, mv = _mem_kv(mem, g_mem[0], w_mem_b, g_mk[0])
        q, k, v, u, cb, mq = _in_proj(x2d, s, g_mix[0], w_in_b, g_q[0], g_k[0], g_mq[0], cos, sin, tm)
        y_attn = _attention(q.reshape(b, s, ATTN_WIDTH), k.reshape(b, s, KV_WIDTH),
                            v.reshape(b, s, KV_WIDTH), _tile(s, 256), _tile(s, 512))
        x1, h2_all, info, route_t, counts = _mix_out(
            y_attn.reshape(b * s, ATTN_WIDTH), u, cb, mq, mk, mv, x2d, s, w_o_b, g_out[0], conv_w[0],
            conv_b[0], g_ffn[0], wr_hl, _tile(s, 2 * MIX_SUB_ROWS), counts, row0, h2_all)
        groups.append((x.shape, x1, info, route_t, tm, row0))
        row0 += b * s

    route_t = jnp.concatenate([g[3] for g in groups], axis=1).astype(jnp.int32)
    experts_flat = route_t[0:TOP_K].reshape(-1)
    rank_flat = route_t[4:4 + TOP_K].reshape(-1)
    pos, sorted_tokens, plan = _dispatch_plan(
        experts_flat, rank_flat, counts[0, :N_EXPERTS].astype(jnp.int32))
    ys = _experts(plan, sorted_tokens, h2_all, w_gate[0], w_up[0], w_down[0])

    outs = []
    for shape, x1, info, _, tm, tok0 in groups:
        outs.append(_combine(pos, x1, info, ys, tm, tok0).reshape(shape))
    return tuple(outs)
```

```python
import functools

import jax
import jax.numpy as jnp
from jax import lax
from jax.experimental import pallas as pl
from jax.experimental.pallas import tpu as pltpu

D_MODEL = 2048
HEAD_DIM = 128
N_Q_HEADS = 8
N_KV_HEADS = 2
Q_PER_KV = N_Q_HEADS // N_KV_HEADS
ATTN_WIDTH = N_Q_HEADS * HEAD_DIM
KV_WIDTH = N_KV_HEADS * HEAD_DIM
CONV_WIDTH = 512
N_MEM_HEADS = 4
MEM_WIDTH = N_MEM_HEADS * HEAD_DIM
MIX_WIDTH = ATTN_WIDTH + CONV_WIDTH + MEM_WIDTH
GRID_W = 64
ROPE_AXIS_DIM = HEAD_DIM // 2
ROPE_THETA = 10000.0
N_GROUPS = 8
EXPERTS_PER_GROUP = 8
N_EXPERTS = N_GROUPS * EXPERTS_PER_GROUP
TOP_K = 2
D_FF_EXPERT = 512
DISPATCH_BLOCK = 256
EPS = 1e-6

LANES = 128
ROUTE_LANES = LANES
ROUTE_ROWS = 8
V7X_VMEM_LIMIT = 56 * 1024 * 1024

F32 = jnp.float32
BF16 = jnp.bfloat16
NEG_BIG = -1e30
LOG2_E = 1.4426950408889634
DMA_ISSUE_UNROLL = 8
N_DMA_QUEUES = 2
GATHER_AHEAD = 4
N_GATHER_BUFS = GATHER_AHEAD + 1
MIX_SUB_ROWS = 256


def _rms(x, g):
    return x * lax.rsqrt(jnp.mean(x * x, axis=-1, keepdims=True) + EPS) * g


def _cparams(sem, **kw):
    return pltpu.CompilerParams(dimension_semantics=sem, vmem_limit_bytes=V7X_VMEM_LIMIT, **kw)


def _mem_kv_kernel(mem_ref, g_ref, w_ref, gk_ref, mk_ref, mv_ref):
    m = _rms(mem_ref[0], g_ref[...]).astype(BF16)
    kv = jnp.dot(m, w_ref[...], preferred_element_type=F32)
    for h in range(N_MEM_HEADS):
        sl = slice(h * HEAD_DIM, (h + 1) * HEAD_DIM)
        mk_ref[0, :, sl] = _rms(kv[:, sl], gk_ref[...]).astype(BF16)
    mv_ref[0] = kv[:, MEM_WIDTH:].astype(BF16)


def _mem_kv(mem, g_mem, w_mem_kv, g_mk):
    b, n_mem, d = mem.shape
    out = jax.ShapeDtypeStruct((b, n_mem, MEM_WIDTH), BF16)
    return pl.pallas_call(
        _mem_kv_kernel,
        out_shape=(out, out),
        grid=(b,),
        in_specs=[
            pl.BlockSpec((1, n_mem, d), lambda i: (i, 0, 0)),
            pl.BlockSpec((1, d), lambda i: (0, 0)),
            pl.BlockSpec((d, 2 * MEM_WIDTH), lambda i: (0, 0)),
            pl.BlockSpec((1, HEAD_DIM), lambda i: (0, 0)),
        ],
        out_specs=(pl.BlockSpec((1, n_mem, MEM_WIDTH), lambda i: (i, 0, 0)),
                   pl.BlockSpec((1, n_mem, MEM_WIDTH), lambda i: (i, 0, 0))),
        compiler_params=_cparams(("arbitrary",)),
        name="mem_kv",
    )(mem, g_mem.reshape(1, d), w_mem_kv, g_mk.reshape(1, HEAD_DIM))


def _rope_tables(seq_len):
    t = jnp.arange(seq_len, dtype=jnp.int32)
    row = (t // GRID_W).astype(F32)
    col = (t % GRID_W).astype(F32)
    inv_freq = ROPE_THETA ** (-jnp.arange(0, ROPE_AXIS_DIM, 2, dtype=F32) / ROPE_AXIS_DIM)
    ang_r = row[:, None] * inv_freq
    ang_c = col[:, None] * inv_freq
    cos = jnp.concatenate([jnp.cos(ang_r), jnp.cos(ang_r), jnp.cos(ang_c), jnp.cos(ang_c)], axis=-1)
    sin = jnp.concatenate([-jnp.sin(ang_r), jnp.sin(ang_r), -jnp.sin(ang_c), jnp.sin(ang_c)], axis=-1)
    return cos, sin


def _in_proj_kernel(x_ref, gmix_ref, w_ref, gq_ref, gk_ref, gmq_ref, cos_ref, sin_ref,
                    q_ref, k_ref, v_ref, u_ref, cb_ref, mq_ref):
    tm = x_ref.shape[0]
    hb = _rms(x_ref[...], gmix_ref[...]).astype(BF16)
    cos = cos_ref[...]
    sin = sin_ref[...]
    lane = lax.broadcasted_iota(jnp.int32, (tm, HEAD_DIM), 1)
    first_half = (lane % (ROPE_AXIS_DIM)) < (ROPE_AXIS_DIM // 2)
    scale = HEAD_DIM ** -0.5
    q_scale = scale * LOG2_E

    def rope(p):
        swapped = jnp.where(first_half, pltpu.roll(p, HEAD_DIM - ROPE_AXIS_DIM // 2, 1),
                            pltpu.roll(p, ROPE_AXIS_DIM // 2, 1))
        return p * cos + swapped * sin

    def proj(lo, width):
        return jnp.dot(hb, w_ref[:, lo:lo + width], preferred_element_type=F32)

    pq = proj(0, ATTN_WIDTH)
    for h in range(N_Q_HEADS):
        sl = slice(h * HEAD_DIM, (h + 1) * HEAD_DIM)
        q_ref[:, sl] = (rope(_rms(pq[:, sl], gq_ref[...])) * q_scale).astype(BF16)
    pkv = proj(ATTN_WIDTH, 2 * KV_WIDTH)
    for h in range(N_KV_HEADS):
        sl = slice(h * HEAD_DIM, (h + 1) * HEAD_DIM)
        k_ref[:, sl] = rope(_rms(pkv[:, sl], gk_ref[...])).astype(BF16)
    v_ref[...] = pkv[:, KV_WIDTH:].astype(BF16)
    pc = proj(ATTN_WIDTH + 2 * KV_WIDTH, 3 * CONV_WIDTH)
    u_ref[...] = (pc[:, 2 * CONV_WIDTH:] * pc[:, :CONV_WIDTH]).astype(BF16)
    cb_ref[...] = pc[:, CONV_WIDTH:2 * CONV_WIDTH].astype(BF16)
    pm = proj(ATTN_WIDTH + 2 * KV_WIDTH + 3 * CONV_WIDTH, MEM_WIDTH)
    for h in range(N_MEM_HEADS):
        sl = slice(h * HEAD_DIM, (h + 1) * HEAD_DIM)
        mq_ref[:, sl] = (_rms(pm[:, sl], gmq_ref[...]) * scale).astype(BF16)


def _in_proj(x2d, seq_len, g_mix, w_in_b, g_q, g_k, g_mq, cos, sin, tm):
    m, d = x2d.shape
    in_width = w_in_b.shape[1]
    tiles_per_seq = seq_len // tm
    row = lambda i: (i, 0)
    const = lambda i: (0, 0)
    pos = lambda i: (i % tiles_per_seq, 0)
    outs = [(ATTN_WIDTH,), (KV_WIDTH,), (KV_WIDTH,), (CONV_WIDTH,), (CONV_WIDTH,), (MEM_WIDTH,)]
    return pl.pallas_call(
        _in_proj_kernel,
        out_shape=tuple(jax.ShapeDtypeStruct((m, w[0]), BF16) for w in outs),
        grid=(m // tm,),
        in_specs=[
            pl.BlockSpec((tm, d), row),
            pl.BlockSpec((1, d), const),
            pl.BlockSpec((d, in_width), const, pipeline_mode=pl.Buffered(1)),
            pl.BlockSpec((1, HEAD_DIM), const),
            pl.BlockSpec((1, HEAD_DIM), const),
            pl.BlockSpec((1, HEAD_DIM), const),
            pl.BlockSpec((tm, HEAD_DIM), pos),
            pl.BlockSpec((tm, HEAD_DIM), pos),
        ],
        out_specs=tuple(pl.BlockSpec((tm, w[0]), row) for w in outs),
        compiler_params=_cparams(("parallel",)),
        name="in_proj",
    )(x2d, g_mix.reshape(1, d), w_in_b, g_q.reshape(1, HEAD_DIM), g_k.reshape(1, HEAD_DIM),
      g_mq.reshape(1, HEAD_DIM), cos, sin)


def _attn_kernel(q_ref, k_ref, v_ref, o_ref, s_a, s_b, acc_ref, *, tk):
    tq = q_ref.shape[1]
    seq = k_ref.shape[1]
    n = seq // tk
    rows = Q_PER_KV * tq
    q = jnp.concatenate([q_ref[0, :, g * HEAD_DIM:(g + 1) * HEAD_DIM] for g in range(Q_PER_KV)], axis=0)
    ones = jnp.ones((tk, HEAD_DIM), BF16)

    def scores(j, s_ref):
        s = lax.dot_general(q, k_ref[0, j * tk:(j + 1) * tk, :], (((1,), (1,)), ((), ())),
                            preferred_element_type=F32)
        s_ref[...] = s
        return jnp.max(s, axis=1, keepdims=True)

    def update(j, s_ref, mx, m, l):
        m_new = jnp.maximum(m, mx)
        alpha = jnp.exp2(m - m_new)
        p = jnp.exp2(s_ref[...] - m_new).astype(BF16)
        v1 = jnp.concatenate([v_ref[0, j * tk:(j + 1) * tk, :], ones], axis=1)
        pv = jnp.dot(p, v1, preferred_element_type=F32)
        acc_ref[...] = alpha * acc_ref[...] + pv[:, :HEAD_DIM]
        return m_new, alpha * l + pv[:, HEAD_DIM:]

    acc_ref[...] = jnp.zeros_like(acc_ref)
    m0 = jnp.full((rows, 1), NEG_BIG, F32)
    l0 = jnp.zeros((rows, HEAD_DIM), F32)
    bufs = (s_a, s_b)
    m, l = m0, l0
    mx = scores(0, bufs[0])
    for j in range(n):
        mx_next = scores(j + 1, bufs[(j + 1) % 2]) if j + 1 < n else None
        m, l = update(j, bufs[j % 2], mx, m, l)
        mx = mx_next
    o = acc_ref[...] / l
    for g in range(Q_PER_KV):
        o_ref[0, :, g * HEAD_DIM:(g + 1) * HEAD_DIM] = o[g * tq:(g + 1) * tq].astype(BF16)


def _attention(q, k, v, tq, tk):
    b, s, _ = q.shape
    grp_w = Q_PER_KV * HEAD_DIM
    rows = Q_PER_KV * tq
    return pl.pallas_call(
        functools.partial(_attn_kernel, tk=tk),
        out_shape=jax.ShapeDtypeStruct((b, s, ATTN_WIDTH), BF16),
        grid=(b, N_KV_HEADS, s // tq),
        in_specs=[
            pl.BlockSpec((1, tq, grp_w), lambda bi, h, qi: (bi, qi, h)),
            pl.BlockSpec((1, s, HEAD_DIM), lambda bi, h, qi: (bi, 0, h)),
            pl.BlockSpec((1, s, HEAD_DIM), lambda bi, h, qi: (bi, 0, h)),
        ],
        out_specs=pl.BlockSpec((1, tq, grp_w), lambda bi, h, qi: (bi, qi, h)),
        scratch_shapes=[pltpu.VMEM((rows, tk), F32), pltpu.VMEM((rows, tk), F32),
                        pltpu.VMEM((rows, HEAD_DIM), F32)],
        compiler_params=_cparams(("parallel", "parallel", "parallel")),
        name="attention",
    )(q, k, v)


def _route(logits):
    tm = logits.shape[0]
    lane = lax.broadcasted_iota(jnp.int32, (tm, ROUTE_LANES), 1)
    lane_f = lane.astype(F32)
    no_lane = float(ROUTE_LANES)
    is_g = lane < N_GROUPS
    gl = jnp.where(is_g, logits, NEG_BIG)
    gmax = jnp.max(gl, axis=1, keepdims=True)
    gsum = jnp.sum(jnp.where(is_g, jnp.exp(gl - gmax), 0.0), axis=1, keepdims=True)
    p_group = 1.0 / gsum
    gidx = jnp.min(jnp.where(is_g & (gl == gmax), lane_f, no_lane), axis=1, keepdims=True)
    lo = N_GROUPS + EXPERTS_PER_GROUP * gidx
    in_grp = (lane_f >= lo) & (lane_f < lo + EXPERTS_PER_GROUP)
    el = jnp.where(in_grp, logits, NEG_BIG)
    emax = jnp.max(el, axis=1, keepdims=True)
    j1 = jnp.min(jnp.where(in_grp & (el == emax), lane_f, no_lane), axis=1, keepdims=True)
    rest = in_grp & (lane_f != j1)
    el2 = jnp.where(rest, logits, NEG_BIG)
    e2max = jnp.max(el2, axis=1, keepdims=True)
    j2 = jnp.min(jnp.where(rest & (el2 == e2max), lane_f, no_lane), axis=1, keepdims=True)
    r = jnp.exp(e2max - emax)
    g1 = p_group / (1.0 + r)
    g2 = g1 * r
    return j1 - N_GROUPS, j2 - N_GROUPS, g1, g2


def _slot_ranks(e1, e2, counts):
    tm = e1.shape[0]
    lane_f = lax.broadcasted_iota(jnp.int32, (tm, ROUTE_LANES), 1).astype(F32)
    oh1 = lane_f == e1
    oh2 = lane_f == e2
    both = jnp.where(oh1 | oh2, 1.0, 0.0)
    earlier = (lax.broadcasted_iota(jnp.int32, (tm, tm), 1)
               < lax.broadcasted_iota(jnp.int32, (tm, tm), 0))
    before = jnp.dot(jnp.where(earlier, 1.0, 0.0).astype(BF16), both.astype(BF16),
                     preferred_element_type=F32) + counts
    rank1 = jnp.sum(jnp.where(oh1, before, 0.0), axis=1, keepdims=True)
    rank2 = jnp.sum(jnp.where(oh2, before, 0.0), axis=1, keepdims=True)
    return rank1, rank2, counts + jnp.sum(both, axis=0, keepdims=True)


def _split_bf16(a):
    hi = a.astype(BF16)
    lo = (a - hi.astype(F32)).astype(BF16)
    return hi, lo


def _mix_out_kernel(ya_ref, u_ref, up_ref, un_ref, cb_ref, mq_ref, mk_ref, mv_ref, x_ref,
                    wo_ref, gout_ref, convw_ref, convb_ref, gffn_ref, wr_hl_ref, cnt_in_ref,
                    h2_buf_ref, x1_ref, h2_ref, info_ref, route_ref, cnt_out_ref, cnt_ref, *, tiles_per_seq):
    del h2_buf_ref
    tm = x_ref.shape[0]
    i = pl.program_id(0)
    t_in_seq = i % tiles_per_seq

    @pl.when(i == 0)
    def _():
        cnt_ref[...] = cnt_in_ref[...]

    u = u_ref[...].astype(F32)
    rowi = lax.broadcasted_iota(jnp.int32, (tm, CONV_WIDTH), 0)
    prev_row = jnp.where(t_in_seq == 0, 0.0, up_ref[7:8, :].astype(F32))
    next_row = jnp.where(t_in_seq == tiles_per_seq - 1, 0.0, un_ref[0:1, :].astype(F32))
    u_prev = jnp.where(rowi == 0, prev_row, pltpu.roll(u, 1, 0))
    u_next = jnp.where(rowi == tm - 1, next_row, pltpu.roll(u, tm - 1, 0))
    conv = u_prev * convw_ref[0:1, :] + u * convw_ref[1:2, :] + u_next * convw_ref[2:3, :] + convb_ref[...]
    c_all = _rms(cb_ref[...].astype(F32) * conv, gout_ref[:, ATTN_WIDTH:ATTN_WIDTH + CONV_WIDTH])

    sub_rows = min(MIX_SUB_ROWS, tm)
    counts = cnt_ref[...]
    for r0 in range(0, tm, sub_rows):
        rs = slice(r0, r0 + sub_rows)
        a = _rms(ya_ref[rs, :].astype(F32), gout_ref[:, :ATTN_WIDTH])
        heads = []
        for h in range(N_MEM_HEADS):
            sl = slice(h * HEAD_DIM, (h + 1) * HEAD_DIM)
            s = lax.dot_general(mq_ref[rs, sl], mk_ref[0, :, sl], (((1,), (1,)), ((), ())),
                                preferred_element_type=F32)
            p = jnp.exp(s - jnp.max(s, axis=1, keepdims=True))
            o = jnp.dot(p.astype(BF16), mv_ref[0, :, sl], preferred_element_type=F32)
            heads.append(o / jnp.sum(p, axis=1, keepdims=True))
        ym = _rms(jnp.concatenate(heads, axis=1), gout_ref[:, ATTN_WIDTH + CONV_WIDTH:])

        mixed = jnp.concatenate([a, c_all[rs], ym], axis=1).astype(BF16)
        x1 = x_ref[rs, :] + jnp.dot(mixed, wo_ref[...], preferred_element_type=F32)
        x1_ref[rs, :] = x1
        h2 = _rms(x1, gffn_ref[...])
        h2_ref[rs, :] = h2
        h_hi, h_lo = _split_bf16(h2)
        hh = jnp.dot(h_hi, wr_hl_ref[...], preferred_element_type=F32)
        logits = (hh[:, :ROUTE_LANES] + hh[:, ROUTE_LANES:]
                  + jnp.dot(h_lo, wr_hl_ref[:, :ROUTE_LANES], preferred_element_type=F32))
        e1, e2, g1, g2 = _route(logits)
        rank1, rank2, counts = _slot_ranks(e1, e2, counts)
        lane = lax.broadcasted_iota(jnp.int32, (sub_rows, ROUTE_LANES), 1)
        info = jnp.zeros((sub_rows, ROUTE_LANES), F32)
        for k, col in enumerate((e1, e2, g1, g2, rank1, rank2)):
            info = jnp.where(lane == k, col, info)
        info_ref[rs, :] = info
        route_ref[:, rs] = info.T[:ROUTE_ROWS, :]
    cnt_ref[...] = counts
    cnt_out_ref[...] = counts


def _mix_out(y_attn, u, cb, mq, mk, mv, x2d, seq_len, w_o_b, g_out, conv_w, conv_b, g_ffn,
             wr_hl, tm, counts_in, h2_row0, h2_buf):
    m, d = x2d.shape
    n_mem = mk.shape[1]
    tiles_per_seq = seq_len // tm
    sub = tm // 8
    n_sub = m // 8
    h2_tile0 = h2_row0 // tm
    row = lambda i: (i, 0)
    const = lambda i: (0, 0)
    batch = lambda i: (i // tiles_per_seq, 0, 0)
    h2_buf_arg = 16
    extra_specs, extra_args = [pl.BlockSpec(memory_space=pl.ANY)], [h2_buf]
    return pl.pallas_call(
        functools.partial(_mix_out_kernel, tiles_per_seq=tiles_per_seq),
        out_shape=(jax.ShapeDtypeStruct((m, d), F32), jax.ShapeDtypeStruct(h2_buf.shape, F32),
                   jax.ShapeDtypeStruct((m, ROUTE_LANES), F32),
                   jax.ShapeDtypeStruct((ROUTE_ROWS, m), F32),
                   jax.ShapeDtypeStruct((1, ROUTE_LANES), F32)),
        grid=(m // tm,),
        input_output_aliases={h2_buf_arg: 1},
        scratch_shapes=[pltpu.VMEM((1, ROUTE_LANES), F32)],
        in_specs=[
            pl.BlockSpec((tm, ATTN_WIDTH), row),
            pl.BlockSpec((tm, CONV_WIDTH), row),
            pl.BlockSpec((8, CONV_WIDTH), lambda i: (jnp.maximum(i * sub - 1, 0), 0)),
            pl.BlockSpec((8, CONV_WIDTH), lambda i: (jnp.minimum((i + 1) * sub, n_sub - 1), 0)),
            pl.BlockSpec((tm, CONV_WIDTH), row),
            pl.BlockSpec((tm, MEM_WIDTH), row),
            pl.BlockSpec((1, n_mem, MEM_WIDTH), batch),
            pl.BlockSpec((1, n_mem, MEM_WIDTH), batch),
            pl.BlockSpec((tm, d), row),
            pl.BlockSpec((MIX_WIDTH, d), const, pipeline_mode=pl.Buffered(1)),
            pl.BlockSpec((1, MIX_WIDTH), const),
            pl.BlockSpec((3, CONV_WIDTH), const),
            pl.BlockSpec((1, CONV_WIDTH), const),
            pl.BlockSpec((1, d), const),
            pl.BlockSpec((d, 2 * ROUTE_LANES), const),
            pl.BlockSpec((1, ROUTE_LANES), const),
        ] + extra_specs,
        out_specs=(pl.BlockSpec((tm, d), row), pl.BlockSpec((tm, d), lambda i: (i + h2_tile0, 0)),
                   pl.BlockSpec((tm, ROUTE_LANES), row), pl.BlockSpec((ROUTE_ROWS, tm), lambda i: (0, i)),
                   pl.BlockSpec((1, ROUTE_LANES), const)),
        compiler_params=_cparams(("arbitrary",)),
        name="mix_out",
    )(y_attn, u, u, u, cb, mq, mk, mv, x2d, w_o_b, g_out.reshape(1, MIX_WIDTH), conv_w,
      conv_b.reshape(1, CONV_WIDTH), g_ffn.reshape(1, d), wr_hl, counts_in, *extra_args)


def _row_copy(src_hbm, idx, dst, row, sem):
    return pltpu.make_async_copy(src_hbm.at[pl.ds(idx, 1), :], dst.at[pl.ds(row, 1), :], sem)


def _experts_kernel(bexp_ref, bfirst_ref, bwslot_ref, bnext_ref, nused_ref, tok_ref,
                    h2_hbm, wg_hbm, wu_hbm, wd_hbm, ys_ref,
                    xbuf, sem, x_b, wg_f, wu_f, wd_f, wsem, wg_b, wu_b, wd_b):
    i = pl.program_id(0)
    n_used = nused_ref[0]
    blk = DISPATCH_BLOCK

    def start_gather(b, slot):
        first = bfirst_ref[b]

        def body(r, carry):
            _row_copy(h2_hbm, tok_ref[first + r], xbuf.at[slot], r, sem.at[slot]).start()
            return carry
        lax.fori_loop(0, blk, body, 0, unroll=DMA_ISSUE_UNROLL)

    def wait_gather(slot):
        pltpu.make_async_copy(h2_hbm.at[pl.ds(0, blk), :], xbuf.at[slot], sem.at[slot]).wait()

    def weight_copies(e, wslot):
        return [pltpu.make_async_copy(src.at[e], dst.at[wslot], wsem.at[wslot])
                for src, dst in ((wg_hbm, wg_f), (wu_hbm, wu_f), (wd_hbm, wd_f))]

    @pl.when(i == 0)
    def _():
        for ahead in range(GATHER_AHEAD):
            start_gather(jnp.minimum(ahead, n_used - 1), ahead)
        for c in weight_copies(bexp_ref[0], 0):
            c.start()

    @pl.when(i < n_used)
    def _():
        slot = i % N_GATHER_BUFS
        wait_gather(slot)

        @pl.when((i == 0) | (bexp_ref[i] != bexp_ref[jnp.maximum(i - 1, 0)]))
        def _():
            wslot = bwslot_ref[i]
            for c in weight_copies(bexp_ref[i], wslot):
                c.wait()
            wg_b[...] = wg_f[wslot].astype(BF16)
            wu_b[...] = wu_f[wslot].astype(BF16)
            wd_b[...] = wd_f[wslot].astype(BF16)

            @pl.when(bnext_ref[i] >= 0)
            def _():
                for c in weight_copies(bnext_ref[i], 1 - wslot):
                    c.start()

        x_b[...] = xbuf[slot].astype(BF16)
        nslot = (i + GATHER_AHEAD) % N_GATHER_BUFS
        nxt = bfirst_ref[jnp.minimum(i + GATHER_AHEAD, n_used - 1)]
        for r in range(blk):
            _row_copy(h2_hbm, tok_ref[nxt + r], xbuf.at[nslot], r,
                      sem.at[nslot]).start(priority=r % N_DMA_QUEUES)

        x = x_b[...]
        g = jnp.dot(x, wg_b[...], preferred_element_type=F32)
        up = jnp.dot(x, wu_b[...], preferred_element_type=F32)
        hid = (g / (1.0 + jnp.exp(-g))) * up
        ys_ref[...] = jnp.dot(hid.astype(BF16), wd_b[...], preferred_element_type=F32)

        @pl.when(i == n_used - 1)
        def _():
            for ahead in range(1, GATHER_AHEAD + 1):
                wait_gather((i + ahead) % N_GATHER_BUFS)

    @pl.when(i >= n_used)
    def _():
        ys_ref[...] = jnp.zeros_like(ys_ref)


def _experts(plan, sorted_tokens, h2, w_gate, w_up, w_down):
    block_expert, block_first, block_wslot, block_next, n_used = plan
    n_blocks = block_expert.shape[0]
    d = h2.shape[1]
    blk = DISPATCH_BLOCK
    hbm = pl.BlockSpec(memory_space=pl.ANY)
    grid_spec = pltpu.PrefetchScalarGridSpec(
        num_scalar_prefetch=6,
        grid=(n_blocks,),
        in_specs=[hbm, hbm, hbm, hbm],
        out_specs=pl.BlockSpec((blk, d), lambda i, *_: (i, 0)),
        scratch_shapes=[
            pltpu.VMEM((N_GATHER_BUFS, blk, d), F32),
            pltpu.SemaphoreType.DMA((N_GATHER_BUFS,)),
            pltpu.VMEM((blk, d), BF16),
            pltpu.VMEM((2, d, D_FF_EXPERT), F32),
            pltpu.VMEM((2, d, D_FF_EXPERT), F32),
            pltpu.VMEM((2, D_FF_EXPERT, d), F32),
            pltpu.SemaphoreType.DMA((2,)),
            pltpu.VMEM((d, D_FF_EXPERT), BF16),
            pltpu.VMEM((d, D_FF_EXPERT), BF16),
            pltpu.VMEM((D_FF_EXPERT, d), BF16),
        ],
    )
    return pl.pallas_call(
        _experts_kernel,
        out_shape=jax.ShapeDtypeStruct((n_blocks * blk, d), F32),
        grid_spec=grid_spec,
        compiler_params=_cparams(("arbitrary",)),
        name="experts",
    )(block_expert, block_first, block_wslot, block_next, n_used, sorted_tokens, h2, w_gate, w_up, w_down)


def _combine_kernel(pos_ref, x1_ref, info_ref, ys_hbm, out_ref, ybuf, sem, *, n, tok0, m_total):
    i = pl.program_id(0)
    tm = x1_ref.shape[0]

    def start_gather(t, slot):
        base = tok0 + t * tm
        for r in range(tm):
            for k in range(TOP_K):
                _row_copy(ys_hbm, pos_ref[base + (k * m_total + r)], ybuf.at[slot, k], r,
                          sem.at[slot]).start(priority=r % N_DMA_QUEUES)

    def start_gather_rolled(t, slot):
        def body(r, carry):
            for k in range(TOP_K):
                _row_copy(ys_hbm, pos_ref[tok0 + t * tm + r + k * m_total], ybuf.at[slot, k], r,
                          sem.at[slot]).start()
            return carry
        lax.fori_loop(0, tm, body, 0, unroll=DMA_ISSUE_UNROLL // TOP_K)

    @pl.when(i == 0)
    def _():
        for ahead in range(min(GATHER_AHEAD, n)):
            start_gather_rolled(ahead, ahead)

    @pl.when(i + GATHER_AHEAD < n)
    def _():
        start_gather(i + GATHER_AHEAD, (i + GATHER_AHEAD) % N_GATHER_BUFS)

    slot = i % N_GATHER_BUFS
    for k in range(TOP_K):
        pltpu.make_async_copy(ys_hbm.at[pl.ds(0, tm), :], ybuf.at[slot, k], sem.at[slot]).wait()
    info = info_ref[...]
    out_ref[...] = (x1_ref[...] + ybuf[slot, 0] * info[:, 2:3] + ybuf[slot, 1] * info[:, 3:4])


def _combine(pos, x1, info, ys, tm, tok0):
    m, d = x1.shape
    m_total = pos.shape[0] // TOP_K
    grid_spec = pltpu.PrefetchScalarGridSpec(
        num_scalar_prefetch=1,
        grid=(m // tm,),
        in_specs=[
            pl.BlockSpec((tm, d), lambda i, pos: (i, 0)),
            pl.BlockSpec((tm, ROUTE_LANES), lambda i, pos: (i, 0)),
            pl.BlockSpec(memory_space=pl.ANY),
        ],
        out_specs=pl.BlockSpec((tm, d), lambda i, pos: (i, 0)),
        scratch_shapes=[pltpu.VMEM((N_GATHER_BUFS, TOP_K, tm, d), F32),
                        pltpu.SemaphoreType.DMA((N_GATHER_BUFS,))],
    )
    return pl.pallas_call(
        functools.partial(_combine_kernel, n=m // tm, tok0=tok0, m_total=m_total),
        out_shape=jax.ShapeDtypeStruct((m, d), F32),
        grid_spec=grid_spec,
        compiler_params=_cparams(("arbitrary",)),
        name="combine",
    )(pos, x1, info, ys)


def _slot_pos_kernel(start_ref, e_ref, rank_ref, pos_ref):
    e = e_ref[...]
    start = jnp.zeros_like(e)
    for k in range(N_EXPERTS):
        start = jnp.where(e == k, start_ref[k], start)
    pos_ref[...] = start + rank_ref[...]


def _slot_positions(padded_start, experts2d, rank2d):
    whole = pl.BlockSpec(experts2d.shape, lambda i, start: (0, 0))
    return pl.pallas_call(
        _slot_pos_kernel,
        out_shape=jax.ShapeDtypeStruct(experts2d.shape, jnp.int32),
        grid_spec=pltpu.PrefetchScalarGridSpec(num_scalar_prefetch=1, grid=(1,), in_specs=[whole, whole],
                                               out_specs=whole),
        name="slot_positions",
    )(padded_start, experts2d, rank2d)


def _dispatch_plan(experts_flat, rank_flat, counts):
    n_slots = experts_flat.shape[0]
    blk = DISPATCH_BLOCK
    n_blocks = -(-n_slots // blk) + N_EXPERTS
    rank = rank_flat
    padded = ((counts + blk - 1) // blk) * blk
    padded_end = jnp.cumsum(padded)
    padded_start = padded_end - padded
    pos = _slot_positions(padded_start.astype(jnp.int32), experts_flat.reshape(-1, LANES),
                          rank.reshape(-1, LANES)).reshape(-1)
    sorted_tokens = jnp.concatenate([jnp.argsort(pos).astype(jnp.int32) % (n_slots // TOP_K),
                                     jnp.zeros((blk,), jnp.int32)])
    n_used = (padded_end[-1] // blk).astype(jnp.int32)
    block_start = jnp.minimum(jnp.arange(n_blocks, dtype=jnp.int32) * blk,
                              jnp.maximum(padded_end[-1] - blk, 0))
    block_expert = jnp.sum(padded_end[None, :] <= block_start[:, None], axis=1)
    block_expert = jnp.minimum(block_expert, N_EXPERTS - 1).astype(jnp.int32)
    start = jnp.cumsum(counts) - counts
    is_expert = block_expert[:, None] == jnp.arange(N_EXPERTS, dtype=jnp.int32)[None, :]

    def per_block(table):
        return jnp.sum(jnp.where(is_expert, table[None, :], 0), axis=1).astype(jnp.int32)

    block_first = block_start - per_block(padded_start - start)
    changed = jnp.concatenate([jnp.zeros((1,), jnp.int32),
                               (block_expert[1:] != block_expert[:-1]).astype(jnp.int32)])
    block_wslot = (jnp.cumsum(changed) % 2).astype(jnp.int32)
    none = N_EXPERTS
    ids = jnp.where(counts > 0, jnp.arange(N_EXPERTS, dtype=jnp.int32), none)
    next_incl = lax.cummin(ids, axis=0, reverse=True)
    next_excl = jnp.concatenate([next_incl[1:], jnp.full((1,), none, jnp.int32)])
    block_next = per_block(jnp.where(next_excl < none, next_excl, -1))
    return pos, sorted_tokens, (block_expert, block_first, block_wslot, block_next, n_used.reshape(1))


def _tile(n, pref):
    return pref if n % pref == 0 else n


def kernel(x_prompt, x_sample, mem_prompt, mem_sample, g_mix, w_in, g_q, g_k, conv_w, conv_b, g_mem, w_mem_kv, g_mq, g_mk, g_out, w_o, g_ffn, w_route_group, w_route_expert, w_gate, w_up, w_down):
    assert g_mix.shape[0] == 1, "single layer"
    d = D_MODEL
    w_in_b = w_in[0].astype(BF16)
    w_o_b = w_o[0].astype(BF16)
    w_mem_b = w_mem_kv[0].astype(BF16)
    w_route = jnp.concatenate(
        [w_route_group[0], w_route_expert[0],
         jnp.zeros((d, ROUTE_LANES - N_GROUPS - N_EXPERTS), F32)], axis=1)
    wr_hl = jnp.concatenate(_split_bf16(w_route), axis=1)

    groups = []
    m_total = x_prompt.shape[0] * x_prompt.shape[1] + x_sample.shape[0] * x_sample.shape[1]
    counts = jnp.zeros((1, ROUTE_LANES), F32)
    h2_all = jnp.zeros((m_total, d), F32)
    cos, sin = _rope_tables(max(x_prompt.shape[1], x_sample.shape[1]))
    row0 = 0
    for x, mem in ((x_prompt, mem_prompt), (x_sample, mem_sample)):
        b, s, _ = x.shape
        tm = _tile(s, 256)
        x2d = x.reshape(b * s, d)
        mk, mv = _mem_kv(mem, g_mem[0], w_mem_b, g_mk[0])
        q, k, v, u, cb, mq = _in_proj(x2d, s, g_mix[0], w_in_b, g_q[0], g_k[0], g_mq[0], cos, sin,
                                      _tile(s, 512))
        y_attn = _attention(q.reshape(b, s, ATTN_WIDTH), k.reshape(b, s, KV_WIDTH),
                            v.reshape(b, s, KV_WIDTH), _tile(s, 512), _tile(s, 512))
        x1, h2_all, info, route_t, counts = _mix_out(
            y_attn.reshape(b * s, ATTN_WIDTH), u, cb, mq, mk, mv, x2d, s, w_o_b, g_out[0], conv_w[0],
            conv_b[0], g_ffn[0], wr_hl, _tile(s, 2 * MIX_SUB_ROWS), counts, row0, h2_all)
        groups.append((x.shape, x1, info, route_t, tm, row0))
        row0 += b * s

    route_t = jnp.concatenate([g[3] for g in groups], axis=1).astype(jnp.int32)
    experts_flat = route_t[0:TOP_K].reshape(-1)
    rank_flat = route_t[4:4 + TOP_K].reshape(-1)
    pos, sorted_tokens, plan = _dispatch_plan(
        experts_flat, rank_flat, counts[0, :N_EXPERTS].astype(jnp.int32))
    ys = _experts(plan, sorted_tokens, h2_all, w_gate[0], w_up[0], w_down[0])

    outs = []
    for shape, x1, info, _, tm, tok0 in groups:
        outs.append(_combine(pos, x1, info, ys, tm, tok0).reshape(shape))
    return tuple(outs)
```

```python
import functools

import jax
import jax.numpy as jnp
from jax import lax
from jax.experimental import pallas as pl
from jax.experimental.pallas import tpu as pltpu

D_MODEL = 2048
HEAD_DIM = 128
N_Q_HEADS = 8
N_KV_HEADS = 2
Q_PER_KV = N_Q_HEADS // N_KV_HEADS
ATTN_WIDTH = N_Q_HEADS * HEAD_DIM
KV_WIDTH = N_KV_HEADS * HEAD_DIM
CONV_WIDTH = 512
N_MEM_HEADS = 4
MEM_WIDTH = N_MEM_HEADS * HEAD_DIM
MIX_WIDTH = ATTN_WIDTH + CONV_WIDTH + MEM_WIDTH
GRID_W = 64
ROPE_AXIS_DIM = HEAD_DIM // 2
ROPE_THETA = 10000.0
N_GROUPS = 8
EXPERTS_PER_GROUP = 8
N_EXPERTS = N_GROUPS * EXPERTS_PER_GROUP
TOP_K = 2
D_FF_EXPERT = 512
DISPATCH_BLOCK = 256
EPS = 1e-6

LANES = 128
ROUTE_LANES = LANES
ROUTE_ROWS = 8
V7X_VMEM_LIMIT = 56 * 1024 * 1024

F32 = jnp.float32
BF16 = jnp.bfloat16
NEG_BIG = -1e30
LOG2_E = 1.4426950408889634
DMA_ISSUE_UNROLL = 8
N_DMA_QUEUES = 2
GATHER_AHEAD = 3
N_GATHER_BUFS = GATHER_AHEAD + 1
MIX_SUB_ROWS = 256


def _rms(x, g):
    return x * lax.rsqrt(jnp.mean(x * x, axis=-1, keepdims=True) + EPS) * g


def _cparams(sem, **kw):
    return pltpu.CompilerParams(dimension_semantics=sem, vmem_limit_bytes=V7X_VMEM_LIMIT, **kw)


def _mem_kv_kernel(mem_ref, g_ref, w_ref, gk_ref, mk_ref, mv_ref):
    m = _rms(mem_ref[0], g_ref[...]).astype(BF16)
    kv = jnp.dot(m, w_ref[...], preferred_element_type=F32)
    for h in range(N_MEM_HEADS):
        sl = slice(h * HEAD_DIM, (h + 1) * HEAD_DIM)
        mk_ref[0, :, sl] = _rms(kv[:, sl], gk_ref[...]).astype(BF16)
    mv_ref[0] = kv[:, MEM_WIDTH:].astype(BF16)


def _mem_kv(mem, g_mem, w_mem_kv, g_mk):
    b, n_mem, d = mem.shape
    out = jax.ShapeDtypeStruct((b, n_mem, MEM_WIDTH), BF16)
    return pl.pallas_call(
        _mem_kv_kernel,
        out_shape=(out, out),
        grid=(b,),
        in_specs=[
            pl.BlockSpec((1, n_mem, d), lambda i: (i, 0, 0)),
            pl.BlockSpec((1, d), lambda i: (0, 0)),
            pl.BlockSpec((d, 2 * MEM_WIDTH), lambda i: (0, 0)),
            pl.BlockSpec((1, HEAD_DIM), lambda i: (0, 0)),
        ],
        out_specs=(pl.BlockSpec((1, n_mem, MEM_WIDTH), lambda i: (i, 0, 0)),
                   pl.BlockSpec((1, n_mem, MEM_WIDTH), lambda i: (i, 0, 0))),
        compiler_params=_cparams(("arbitrary",)),
        name="mem_kv",
    )(mem, g_mem.reshape(1, d), w_mem_kv, g_mk.reshape(1, HEAD_DIM))


def _rope_tables(seq_len):
    t = jnp.arange(seq_len, dtype=jnp.int32)
    row = (t // GRID_W).astype(F32)
    col = (t % GRID_W).astype(F32)
    inv_freq = ROPE_THETA ** (-jnp.arange(0, ROPE_AXIS_DIM, 2, dtype=F32) / ROPE_AXIS_DIM)
    ang_r = row[:, None] * inv_freq
    ang_c = col[:, None] * inv_freq
    cos = jnp.concatenate([jnp.cos(ang_r), jnp.cos(ang_r), jnp.cos(ang_c), jnp.cos(ang_c)], axis=-1)
    sin = jnp.concatenate([-jnp.sin(ang_r), jnp.sin(ang_r), -jnp.sin(ang_c), jnp.sin(ang_c)], axis=-1)
    return cos, sin


def _in_proj_kernel(x_ref, gmix_ref, w_ref, gq_ref, gk_ref, gmq_ref, cos_ref, sin_ref,
                    q_ref, k_ref, v_ref, u_ref, cb_ref, mq_ref):
    tm = x_ref.shape[0]
    hb = _rms(x_ref[...], gmix_ref[...]).astype(BF16)
    cos = cos_ref[...]
    sin = sin_ref[...]
    lane = lax.broadcasted_iota(jnp.int32, (tm, HEAD_DIM), 1)
    first_half = (lane % (ROPE_AXIS_DIM)) < (ROPE_AXIS_DIM // 2)
    scale = HEAD_DIM ** -0.5
    q_scale = scale * LOG2_E

    def rope(p):
        swapped = jnp.where(first_half, pltpu.roll(p, HEAD_DIM - ROPE_AXIS_DIM // 2, 1),
                            pltpu.roll(p, ROPE_AXIS_DIM // 2, 1))
        return p * cos + swapped * sin

    def proj(lo, width):
        return jnp.dot(hb, w_ref[:, lo:lo + width], preferred_element_type=F32)

    pq = proj(0, ATTN_WIDTH)
    for h in range(N_Q_HEADS):
        sl = slice(h * HEAD_DIM, (h + 1) * HEAD_DIM)
        q_ref[:, sl] = (rope(_rms(pq[:, sl], gq_ref[...])) * q_scale).astype(BF16)
    pkv = proj(ATTN_WIDTH, 2 * KV_WIDTH)
    for h in range(N_KV_HEADS):
        sl = slice(h * HEAD_DIM, (h + 1) * HEAD_DIM)
        k_ref[:, sl] = rope(_rms(pkv[:, sl], gk_ref[...])).astype(BF16)
    v_ref[...] = pkv[:, KV_WIDTH:].astype(BF16)
    pc = proj(ATTN_WIDTH + 2 * KV_WIDTH, 3 * CONV_WIDTH)
    u_ref[...] = (pc[:, 2 * CONV_WIDTH:] * pc[:, :CONV_WIDTH]).astype(BF16)
    cb_ref[...] = pc[:, CONV_WIDTH:2 * CONV_WIDTH].astype(BF16)
    pm = proj(ATTN_WIDTH + 2 * KV_WIDTH + 3 * CONV_WIDTH, MEM_WIDTH)
    for h in range(N_MEM_HEADS):
        sl = slice(h * HEAD_DIM, (h + 1) * HEAD_DIM)
        mq_ref[:, sl] = (_rms(pm[:, sl], gmq_ref[...]) * scale).astype(BF16)


def _in_proj(x2d, seq_len, g_mix, w_in_b, g_q, g_k, g_mq, cos, sin, tm):
    m, d = x2d.shape
    in_width = w_in_b.shape[1]
    tiles_per_seq = seq_len // tm
    row = lambda i: (i, 0)
    const = lambda i: (0, 0)
    pos = lambda i: (i % tiles_per_seq, 0)
    outs = [(ATTN_WIDTH,), (KV_WIDTH,), (KV_WIDTH,), (CONV_WIDTH,), (CONV_WIDTH,), (MEM_WIDTH,)]
    return pl.pallas_call(
        _in_proj_kernel,
        out_shape=tuple(jax.ShapeDtypeStruct((m, w[0]), BF16) for w in outs),
        grid=(m // tm,),
        in_specs=[
            pl.BlockSpec((tm, d), row),
            pl.BlockSpec((1, d), const),
            pl.BlockSpec((d, in_width), const, pipeline_mode=pl.Buffered(1)),
            pl.BlockSpec((1, HEAD_DIM), const),
            pl.BlockSpec((1, HEAD_DIM), const),
            pl.BlockSpec((1, HEAD_DIM), const),
            pl.BlockSpec((tm, HEAD_DIM), pos),
            pl.BlockSpec((tm, HEAD_DIM), pos),
        ],
        out_specs=tuple(pl.BlockSpec((tm, w[0]), row) for w in outs),
        compiler_params=_cparams(("parallel",)),
        name="in_proj",
    )(x2d, g_mix.reshape(1, d), w_in_b, g_q.reshape(1, HEAD_DIM), g_k.reshape(1, HEAD_DIM),
      g_mq.reshape(1, HEAD_DIM), cos, sin)


def _attn_kernel(q_ref, k_ref, v_ref, o_ref, s_a, s_b, acc_ref, *, tk):
    tq = q_ref.shape[1]
    seq = k_ref.shape[1]
    n = seq // tk
    rows = Q_PER_KV * tq
    q = jnp.concatenate([q_ref[0, :, g * HEAD_DIM:(g + 1) * HEAD_DIM] for g in range(Q_PER_KV)], axis=0)
    ones = jnp.ones((tk, HEAD_DIM), BF16)

    def scores(j, s_ref):
        s = lax.dot_general(q, k_ref[0, j * tk:(j + 1) * tk, :], (((1,), (1,)), ((), ())),
                            preferred_element_type=F32)
        s_ref[...] = s
        return jnp.max(s, axis=1, keepdims=True)

    def update(j, s_ref, mx, m, l):
        m_new = jnp.maximum(m, mx)
        alpha = jnp.exp2(m - m_new)
        p = jnp.exp2(s_ref[...] - m_new).astype(BF16)
        v1 = jnp.concatenate([v_ref[0, j * tk:(j + 1) * tk, :], ones], axis=1)
        pv = jnp.dot(p, v1, preferred_element_type=F32)
        acc_ref[...] = alpha * acc_ref[...] + pv[:, :HEAD_DIM]
        return m_new, alpha * l + pv[:, HEAD_DIM:]

    acc_ref[...] = jnp.zeros_like(acc_ref)
    m0 = jnp.full((rows, 1), NEG_BIG, F32)
    l0 = jnp.zeros((rows, HEAD_DIM), F32)
    bufs = (s_a, s_b)
    m, l = m0, l0
    mx = scores(0, bufs[0])
    for j in range(n):
        mx_next = scores(j + 1, bufs[(j + 1) % 2]) if j + 1 < n else None
        m, l = update(j, bufs[j % 2], mx, m, l)
        mx = mx_next
    o = acc_ref[...] / l
    for g in range(Q_PER_KV):
        o_ref[0, :, g * HEAD_DIM:(g + 1) * HEAD_DIM] = o[g * tq:(g + 1) * tq].astype(BF16)


def _attention(q, k, v, tq, tk):
    b, s, _ = q.shape
    grp_w = Q_PER_KV * HEAD_DIM
    rows = Q_PER_KV * tq
    return pl.pallas_call(
        functools.partial(_attn_kernel, tk=tk),
        out_shape=jax.ShapeDtypeStruct((b, s, ATTN_WIDTH), BF16),
        grid=(b, N_KV_HEADS, s // tq),
        in_specs=[
            pl.BlockSpec((1, tq, grp_w), lambda bi, h, qi: (bi, qi, h)),
            pl.BlockSpec((1, s, HEAD_DIM), lambda bi, h, qi: (bi, 0, h)),
            pl.BlockSpec((1, s, HEAD_DIM), lambda bi, h, qi: (bi, 0, h)),
        ],
        out_specs=pl.BlockSpec((1, tq, grp_w), lambda bi, h, qi: (bi, qi, h)),
        scratch_shapes=[pltpu.VMEM((rows, tk), F32), pltpu.VMEM((rows, tk), F32),
                        pltpu.VMEM((rows, HEAD_DIM), F32)],
        compiler_params=_cparams(("parallel", "parallel", "parallel")),
        name="attention",
    )(q, k, v)


def _route(logits):
    tm = logits.shape[0]
    lane = lax.broadcasted_iota(jnp.int32, (tm, ROUTE_LANES), 1)
    lane_f = lane.astype(F32)
    no_lane = float(ROUTE_LANES)
    is_g = lane < N_GROUPS
    gl = jnp.where(is_g, logits, NEG_BIG)
    gmax = jnp.max(gl, axis=1, keepdims=True)
    gsum = jnp.sum(jnp.where(is_g, jnp.exp(gl - gmax), 0.0), axis=1, keepdims=True)
    p_group = 1.0 / gsum
    gidx = jnp.min(jnp.where(is_g & (gl == gmax), lane_f, no_lane), axis=1, keepdims=True)
    lo = N_GROUPS + EXPERTS_PER_GROUP * gidx
    in_grp = (lane_f >= lo) & (lane_f < lo + EXPERTS_PER_GROUP)
    el = jnp.where(in_grp, logits, NEG_BIG)
    emax = jnp.max(el, axis=1, keepdims=True)
    j1 = jnp.min(jnp.where(in_grp & (el == emax), lane_f, no_lane), axis=1, keepdims=True)
    rest = in_grp & (lane_f != j1)
    el2 = jnp.where(rest, logits, NEG_BIG)
    e2max = jnp.max(el2, axis=1, keepdims=True)
    j2 = jnp.min(jnp.where(rest & (el2 == e2max), lane_f, no_lane), axis=1, keepdims=True)
    r = jnp.exp(e2max - emax)
    g1 = p_group / (1.0 + r)
    g2 = g1 * r
    return j1 - N_GROUPS, j2 - N_GROUPS, g1, g2


def _slot_ranks(e1, e2, counts):
    tm = e1.shape[0]
    lane_f = lax.broadcasted_iota(jnp.int32, (tm, ROUTE_LANES), 1).astype(F32)
    oh1 = lane_f == e1
    oh2 = lane_f == e2
    both = jnp.where(oh1 | oh2, 1.0, 0.0)
    earlier = (lax.broadcasted_iota(jnp.int32, (tm, tm), 1)
               < lax.broadcasted_iota(jnp.int32, (tm, tm), 0))
    before = jnp.dot(jnp.where(earlier, 1.0, 0.0).astype(BF16), both.astype(BF16),
                     preferred_element_type=F32) + counts
    rank1 = jnp.sum(jnp.where(oh1, before, 0.0), axis=1, keepdims=True)
    rank2 = jnp.sum(jnp.where(oh2, before, 0.0), axis=1, keepdims=True)
    return rank1, rank2, counts + jnp.sum(both, axis=0, keepdims=True)


def _split_bf16(a):
    hi = a.astype(BF16)
    lo = (a - hi.astype(F32)).astype(BF16)
    return hi, lo


def _mix_out_kernel(ya_ref, u_ref, up_ref, un_ref, cb_ref, mq_ref, mk_ref, mv_ref, x_ref,
                    wo_ref, gout_ref, convw_ref, convb_ref, gffn_ref, wr_hl_ref, cnt_in_ref,
                    h2_buf_ref, x1_ref, h2_ref, info_ref, route_ref, cnt_out_ref, cnt_ref, *, tiles_per_seq):
    del h2_buf_ref
    tm = x_ref.shape[0]
    i = pl.program_id(0)
    t_in_seq = i % tiles_per_seq

    @pl.when(i == 0)
    def _():
        cnt_ref[...] = cnt_in_ref[...]

    u = u_ref[...].astype(F32)
    rowi = lax.broadcasted_iota(jnp.int32, (tm, CONV_WIDTH), 0)
    prev_row = jnp.where(t_in_seq == 0, 0.0, up_ref[7:8, :].astype(F32))
    next_row = jnp.where(t_in_seq == tiles_per_seq - 1, 0.0, un_ref[0:1, :].astype(F32))
    u_prev = jnp.where(rowi == 0, prev_row, pltpu.roll(u, 1, 0))
    u_next = jnp.where(rowi == tm - 1, next_row, pltpu.roll(u, tm - 1, 0))
    conv = u_prev * convw_ref[0:1, :] + u * convw_ref[1:2, :] + u_next * convw_ref[2:3, :] + convb_ref[...]
    c_all = _rms(cb_ref[...].astype(F32) * conv, gout_ref[:, ATTN_WIDTH:ATTN_WIDTH + CONV_WIDTH])

    sub_rows = min(MIX_SUB_ROWS, tm)
    counts = cnt_ref[...]
    for r0 in range(0, tm, sub_rows):
        rs = slice(r0, r0 + sub_rows)
        a = _rms(ya_ref[rs, :].astype(F32), gout_ref[:, :ATTN_WIDTH])
        heads = []
        for h in range(N_MEM_HEADS):
            sl = slice(h * HEAD_DIM, (h + 1) * HEAD_DIM)
            s = lax.dot_general(mq_ref[rs, sl], mk_ref[0, :, sl], (((1,), (1,)), ((), ())),
                                preferred_element_type=F32)
            p = jnp.exp(s - jnp.max(s, axis=1, keepdims=True))
            o = jnp.dot(p.astype(BF16), mv_ref[0, :, sl], preferred_element_type=F32)
            heads.append(o / jnp.sum(p, axis=1, keepdims=True))
        ym = _rms(jnp.concatenate(heads, axis=1), gout_ref[:, ATTN_WIDTH + CONV_WIDTH:])

        mixed = jnp.concatenate([a, c_all[rs], ym], axis=1).astype(BF16)
        x1 = x_ref[rs, :] + jnp.dot(mixed, wo_ref[...], preferred_element_type=F32)
        x1_ref[rs, :] = x1
        h2 = _rms(x1, gffn_ref[...])
        h2_ref[rs, :] = h2
        h_hi, h_lo = _split_bf16(h2)
        hh = jnp.dot(h_hi, wr_hl_ref[...], preferred_element_type=F32)
        logits = (hh[:, :ROUTE_LANES] + hh[:, ROUTE_LANES:]
                  + jnp.dot(h_lo, wr_hl_ref[:, :ROUTE_LANES], preferred_element_type=F32))
        e1, e2, g1, g2 = _route(logits)
        rank1, rank2, counts = _slot_ranks(e1, e2, counts)
        lane = lax.broadcasted_iota(jnp.int32, (sub_rows, ROUTE_LANES), 1)
        info = jnp.zeros((sub_rows, ROUTE_LANES), F32)
        for k, col in enumerate((e1, e2, g1, g2, rank1, rank2)):
            info = jnp.where(lane == k, col, info)
        info_ref[rs, :] = info
        route_ref[:, rs] = info.T[:ROUTE_ROWS, :]
    cnt_ref[...] = counts
    cnt_out_ref[...] = counts


def _mix_out(y_attn, u, cb, mq, mk, mv, x2d, seq_len, w_o_b, g_out, conv_w, conv_b, g_ffn,
             wr_hl, tm, counts_in, h2_row0, h2_buf):
    m, d = x2d.shape
    n_mem = mk.shape[1]
    tiles_per_seq = seq_len // tm
    sub = tm // 8
    n_sub = m // 8
    h2_tile0 = h2_row0 // tm
    row = lambda i: (i, 0)
    const = lambda i: (0, 0)
    batch = lambda i: (i // tiles_per_seq, 0, 0)
    h2_buf_arg = 16
    extra_specs, extra_args = [pl.BlockSpec(memory_space=pl.ANY)], [h2_buf]
    return pl.pallas_call(
        functools.partial(_mix_out_kernel, tiles_per_seq=tiles_per_seq),
        out_shape=(jax.ShapeDtypeStruct((m, d), F32), jax.ShapeDtypeStruct(h2_buf.shape, F32),
                   jax.ShapeDtypeStruct((m, ROUTE_LANES), F32),
                   jax.ShapeDtypeStruct((ROUTE_ROWS, m), F32),
                   jax.ShapeDtypeStruct((1, ROUTE_LANES), F32)),
        grid=(m // tm,),
        input_output_aliases={h2_buf_arg: 1},
        scratch_shapes=[pltpu.VMEM((1, ROUTE_LANES), F32)],
        in_specs=[
            pl.BlockSpec((tm, ATTN_WIDTH), row),
            pl.BlockSpec((tm, CONV_WIDTH), row),
            pl.BlockSpec((8, CONV_WIDTH), lambda i: (jnp.maximum(i * sub - 1, 0), 0)),
            pl.BlockSpec((8, CONV_WIDTH), lambda i: (jnp.minimum((i + 1) * sub, n_sub - 1), 0)),
            pl.BlockSpec((tm, CONV_WIDTH), row),
            pl.BlockSpec((tm, MEM_WIDTH), row),
            pl.BlockSpec((1, n_mem, MEM_WIDTH), batch),
            pl.BlockSpec((1, n_mem, MEM_WIDTH), batch),
            pl.BlockSpec((tm, d), row),
            pl.BlockSpec((MIX_WIDTH, d), const, pipeline_mode=pl.Buffered(1)),
            pl.BlockSpec((1, MIX_WIDTH), const),
            pl.BlockSpec((3, CONV_WIDTH), const),
            pl.BlockSpec((1, CONV_WIDTH), const),
            pl.BlockSpec((1, d), const),
            pl.BlockSpec((d, 2 * ROUTE_LANES), const),
            pl.BlockSpec((1, ROUTE_LANES), const),
        ] + extra_specs,
        out_specs=(pl.BlockSpec((tm, d), row), pl.BlockSpec((tm, d), lambda i: (i + h2_tile0, 0)),
                   pl.BlockSpec((tm, ROUTE_LANES), row), pl.BlockSpec((ROUTE_ROWS, tm), lambda i: (0, i)),
                   pl.BlockSpec((1, ROUTE_LANES), const)),
        compiler_params=_cparams(("arbitrary",)),
        name="mix_out",
    )(y_attn, u, u, u, cb, mq, mk, mv, x2d, w_o_b, g_out.reshape(1, MIX_WIDTH), conv_w,
      conv_b.reshape(1, CONV_WIDTH), g_ffn.reshape(1, d), wr_hl, counts_in, *extra_args)


def _row_copy(src_hbm, idx, dst, row, sem):
    return pltpu.make_async_copy(src_hbm.at[pl.ds(idx, 1), :], dst.at[pl.ds(row, 1), :], sem)


def _experts_kernel(bexp_ref, bfirst_ref, bwslot_ref, bnext_ref, nused_ref, tok_ref,
                    h2_hbm, wg_hbm, wu_hbm, wd_hbm, ys_ref,
                    xbuf, sem, x_b, wg_f, wu_f, wd_f, wsem, wg_b, wu_b, wd_b):
    i = pl.program_id(0)
    n_used = nused_ref[0]
    blk = DISPATCH_BLOCK

    def start_gather(b, slot):
        first = bfirst_ref[b]

        def body(r, carry):
            _row_copy(h2_hbm, tok_ref[first + r], xbuf.at[slot], r, sem.at[slot]).start()
            return carry
        lax.fori_loop(0, blk, body, 0, unroll=DMA_ISSUE_UNROLL)

    def wait_gather(slot):
        pltpu.make_async_copy(h2_hbm.at[pl.ds(0, blk), :], xbuf.at[slot], sem.at[slot]).wait()

    def weight_copies(e, wslot):
        return [pltpu.make_async_copy(src.at[e], dst.at[wslot], wsem.at[wslot])
                for src, dst in ((wg_hbm, wg_f), (wu_hbm, wu_f), (wd_hbm, wd_f))]

    @pl.when(i == 0)
    def _():
        for ahead in range(GATHER_AHEAD):
            start_gather(jnp.minimum(ahead, n_used - 1), ahead)
        for c in weight_copies(bexp_ref[0], 0):
            c.start()

    @pl.when(i < n_used)
    def _():
        slot = i % N_GATHER_BUFS
        wait_gather(slot)

        @pl.when((i == 0) | (bexp_ref[i] != bexp_ref[jnp.maximum(i - 1, 0)]))
        def _():
            wslot = bwslot_ref[i]
            for c in weight_copies(bexp_ref[i], wslot):
                c.wait()
            wg_b[...] = wg_f[wslot].astype(BF16)
            wu_b[...] = wu_f[wslot].astype(BF16)
            wd_b[...] = wd_f[wslot].astype(BF16)

            @pl.when(bnext_ref[i] >= 0)
            def _():
                for c in weight_copies(bnext_ref[i], 1 - wslot):
                    c.start()

        x_b[...] = xbuf[slot].astype(BF16)
        nslot = (i + GATHER_AHEAD) % N_GATHER_BUFS
        nxt = bfirst_ref[jnp.minimum(i + GATHER_AHEAD, n_used - 1)]
        for r in range(blk):
            _row_copy(h2_hbm, tok_ref[nxt + r], xbuf.at[nslot], r,
                      sem.at[nslot]).start(priority=r % N_DMA_QUEUES)

        x = x_b[...]
        g = jnp.dot(x, wg_b[...], preferred_element_type=F32)
        up = jnp.dot(x, wu_b[...], preferred_element_type=F32)
        hid = (g / (1.0 + jnp.exp(-g))) * up
        ys_ref[...] = jnp.dot(hid.astype(BF16), wd_b[...], preferred_element_type=F32)

        @pl.when(i == n_used - 1)
        def _():
            for ahead in range(1, GATHER_AHEAD + 1):
                wait_gather((i + ahead) % N_GATHER_BUFS)

    @pl.when(i >= n_used)
    def _():
        ys_ref[...] = jnp.zeros_like(ys_ref)


def _experts(plan, sorted_tokens, h2, w_gate, w_up, w_down):
    block_expert, block_first, block_wslot, block_next, n_used = plan
    n_blocks = block_expert.shape[0]
    d = h2.shape[1]
    blk = DISPATCH_BLOCK
    hbm = pl.BlockSpec(memory_space=pl.ANY)
    grid_spec = pltpu.PrefetchScalarGridSpec(
        num_scalar_prefetch=6,
        grid=(n_blocks,),
        in_specs=[hbm, hbm, hbm, hbm],
        out_specs=pl.BlockSpec((blk, d), lambda i, *_: (i, 0)),
        scratch_shapes=[
            pltpu.VMEM((N_GATHER_BUFS, blk, d), F32),
            pltpu.SemaphoreType.DMA((N_GATHER_BUFS,)),
            pltpu.VMEM((blk, d), BF16),
            pltpu.VMEM((2, d, D_FF_EXPERT), F32),
            pltpu.VMEM((2, d, D_FF_EXPERT), F32),
            pltpu.VMEM((2, D_FF_EXPERT, d), F32),
            pltpu.SemaphoreType.DMA((2,)),
            pltpu.VMEM((d, D_FF_EXPERT), BF16),
            pltpu.VMEM((d, D_FF_EXPERT), BF16),
            pltpu.VMEM((D_FF_EXPERT, d), BF16),
        ],
    )
    return pl.pallas_call(
        _experts_kernel,
        out_shape=jax.ShapeDtypeStruct((n_blocks * blk, d), F32),
        grid_spec=grid_spec,
        compiler_params=_cparams(("arbitrary",)),
        name="experts",
    )(block_expert, block_first, block_wslot, block_next, n_used, sorted_tokens, h2, w_gate, w_up, w_down)


def _combine_kernel(pos_ref, x1_ref, info_ref, ys_hbm, out_ref, ybuf, sem, *, n, tok0, m_total):
    i = pl.program_id(0)
    tm = x1_ref.shape[0]

    def start_gather(t, slot):
        base = tok0 + t * tm
        for r in range(tm):
            for k in range(TOP_K):
                _row_copy(ys_hbm, pos_ref[base + (k * m_total + r)], ybuf.at[slot, k], r,
                          sem.at[slot]).start(priority=r % N_DMA_QUEUES)

    def start_gather_rolled(t, slot):
        def body(r, carry):
            for k in range(TOP_K):
                _row_copy(ys_hbm, pos_ref[tok0 + t * tm + r + k * m_total], ybuf.at[slot, k], r,
                          sem.at[slot]).start()
            return carry
        lax.fori_loop(0, tm, body, 0, unroll=DMA_ISSUE_UNROLL // TOP_K)

    @pl.when(i == 0)
    def _():
        for ahead in range(min(GATHER_AHEAD, n)):
            start_gather_rolled(ahead, ahead)

    @pl.when(i + GATHER_AHEAD < n)
    def _():
        start_gather(i + GATHER_AHEAD, (i + GATHER_AHEAD) % N_GATHER_BUFS)

    slot = i % N_GATHER_BUFS
    for k in range(TOP_K):
        pltpu.make_async_copy(ys_hbm.at[pl.ds(0, tm), :], ybuf.at[slot, k], sem.at[slot]).wait()
    info = info_ref[...]
    out_ref[...] = (x1_ref[...] + ybuf[slot, 0] * info[:, 2:3] + ybuf[slot, 1] * info[:, 3:4])


def _combine(pos, x1, info, ys, tm, tok0):
    m, d = x1.shape
    m_total = pos.shape[0] // TOP_K
    grid_spec = pltpu.PrefetchScalarGridSpec(
        num_scalar_prefetch=1,
        grid=(m // tm,),
        in_specs=[
            pl.BlockSpec((tm, d), lambda i, pos: (i, 0)),
            pl.BlockSpec((tm, ROUTE_LANES), lambda i, pos: (i, 0)),
            pl.BlockSpec(memory_space=pl.ANY),
        ],
        out_specs=pl.BlockSpec((tm, d), lambda i, pos: (i, 0)),
        scratch_shapes=[pltpu.VMEM((N_GATHER_BUFS, TOP_K, tm, d), F32),
                        pltpu.SemaphoreType.DMA((N_GATHER_BUFS,))],
    )
    return pl.pallas_call(
        functools.partial(_combine_kernel, n=m // tm, tok0=tok0, m_total=m_total),
        out_shape=jax.ShapeDtypeStruct((m, d), F32),
        grid_spec=grid_spec,
        compiler_params=_cparams(("arbitrary",)),
        name="combine",
    )(pos, x1, info, ys)


def _slot_pos_kernel(start_ref, e_ref, rank_ref, pos_ref):
    e = e_ref[...]
    start = jnp.zeros_like(e)
    for k in range(N_EXPERTS):
        start = jnp.where(e == k, start_ref[k], start)
    pos_ref[...] = start + rank_ref[...]


def _slot_positions(padded_start, experts2d, rank2d):
    whole = pl.BlockSpec(experts2d.shape, lambda i, start: (0, 0))
    return pl.pallas_call(
        _slot_pos_kernel,
        out_shape=jax.ShapeDtypeStruct(experts2d.shape, jnp.int32),
        grid_spec=pltpu.PrefetchScalarGridSpec(num_scalar_prefetch=1, grid=(1,), in_specs=[whole, whole],
                                               out_specs=whole),
        name="slot_positions",
    )(padded_start, experts2d, rank2d)


def _dispatch_plan(experts_flat, rank_flat, counts):
    n_slots = experts_flat.shape[0]
    blk = DISPATCH_BLOCK
    n_blocks = -(-n_slots // blk) + N_EXPERTS
    rank = rank_flat
    padded = ((counts + blk - 1) // blk) * blk
    padded_end = jnp.cumsum(padded)
    padded_start = padded_end - padded
    pos = _slot_positions(padded_start.astype(jnp.int32), experts_flat.reshape(-1, LANES),
                          rank.reshape(-1, LANES)).reshape(-1)
    sorted_tokens = jnp.concatenate([jnp.argsort(pos).astype(jnp.int32) % (n_slots // TOP_K),
                                     jnp.zeros((blk,), jnp.int32)])
    n_used = (padded_end[-1] // blk).astype(jnp.int32)
    block_start = jnp.minimum(jnp.arange(n_blocks, dtype=jnp.int32) * blk,
                              jnp.maximum(padded_end[-1] - blk, 0))
    block_expert = jnp.sum(padded_end[None, :] <= block_start[:, None], axis=1)
    block_expert = jnp.minimum(block_expert, N_EXPERTS - 1).astype(jnp.int32)
    start = jnp.cumsum(counts) - counts
    is_expert = block_expert[:, None] == jnp.arange(N_EXPERTS, dtype=jnp.int32)[None, :]

    def per_block(table):
        return jnp.sum(jnp.where(is_expert, table[None, :], 0), axis=1).astype(jnp.int32)

    block_first = block_start - per_block(padded_start - start)
    changed = jnp.concatenate([jnp.zeros((1,), jnp.int32),
                               (block_expert[1:] != block_expert[:-1]).astype(jnp.int32)])
    block_wslot = (jnp.cumsum(changed) % 2).astype(jnp.int32)
    none = N_EXPERTS
    ids = jnp.where(counts > 0, jnp.arange(N_EXPERTS, dtype=jnp.int32), none)
    next_incl = lax.cummin(ids, axis=0, reverse=True)
    next_excl = jnp.concatenate([next_incl[1:], jnp.full((1,), none, jnp.int32)])
    block_next = per_block(jnp.where(next_excl < none, next_excl, -1))
    return pos, sorted_tokens, (block_expert, block_first, block_wslot, block_next, n_used.reshape(1))


def _tile(n, pref):
    return pref if n % pref == 0 else n


def kernel(x_prompt, x_sample, mem_prompt, mem_sample, g_mix, w_in, g_q, g_k, conv_w, conv_b, g_mem, w_mem_kv, g_mq, g_mk, g_out, w_o, g_ffn, w_route_group, w_route_expert, w_gate, w_up, w_down):
    assert g_mix.shape[0] == 1, "single layer"
    d = D_MODEL
    w_in_b = w_in[0].astype(BF16)
    w_o_b = w_o[0].astype(BF16)
    w_mem_b = w_mem_kv[0].astype(BF16)
    w_route = jnp.concatenate(
        [w_route_group[0], w_route_expert[0],
         jnp.zeros((d, ROUTE_LANES - N_GROUPS - N_EXPERTS), F32)], axis=1)
    wr_hl = jnp.concatenate(_split_bf16(w_route), axis=1)

    groups = []
    m_total = x_prompt.shape[0] * x_prompt.shape[1] + x_sample.shape[0] * x_sample.shape[1]
    counts = jnp.zeros((1, ROUTE_LANES), F32)
    h2_all = jnp.zeros((m_total, d), F32)
    cos, sin = _rope_tables(max(x_prompt.shape[1], x_sample.shape[1]))
    row0 = 0
    for x, mem in ((x_prompt, mem_prompt), (x_sample, mem_sample)):
        b, s, _ = x.shape
        tm = _tile(s, 256)
        x2d = x.reshape(b * s, d)
        mk, mv = _mem_kv(mem, g_mem[0], w_mem_b, g_mk[0])
        q, k, v, u, cb, mq = _in_proj(x2d, s, g_mix[0], w_in_b, g_q[0], g_k[0], g_mq[0], cos, sin,
                                      _tile(s, 512))
        y_attn = _attention(q.reshape(b, s, ATTN_WIDTH), k.reshape(b, s, KV_WIDTH),
                            v.reshape(b, s, KV_WIDTH), _tile(s, 256), _tile(s, 512))
        x1, h2_all, info, route_t, counts = _mix_out(
            y_attn.reshape(b * s, ATTN_WIDTH), u, cb, mq, mk, mv, x2d, s, w_o_b, g_out[0], conv_w[0],
            conv_b[0], g_ffn[0], wr_hl, _tile(s, 2 * MIX_SUB_ROWS), counts, row0, h2_all)
        groups.append((x.shape, x1, info, route_t, tm, row0))
        row0 += b * s

    route_t = jnp.concatenate([g[3] for g in groups], axis=1).astype(jnp.int32)
    experts_flat = route_t[0:TOP_K].reshape(-1)
    rank_flat = route_t[4:4 + TOP_K].reshape(-1)
    pos, sorted_tokens, plan = _dispatch_plan(
        experts_flat, rank_flat, counts[0, :N_EXPERTS].astype(jnp.int32))
    ys = _experts(plan, sorted_tokens, h2_all, w_gate[0], w_up[0], w_down[0])

    outs = []
    for shape, x1, info, _, tm, tok0 in groups:
        outs.append(_combine(pos, x1, info, ys, tm, tok0).reshape(shape))
    return tuple(outs)
```
